```python
import jax, jax.numpy as jnp
from jax import lax
import numpy as np

D_MODEL = 1024
BATCH = 4
SEQ = 4096
DEPTH = 1
DEC_BATCH = 128
DEC_SEQ = 1
PAST_LEN = 8192
PAGE_SIZE = 128

HEAD_DIM = 64
SB_HEADS = 8
SB_KV_HEADS = 4
DSA_HEADS = 8
DSA_KV_HEADS = 2
IDX_HEADS = 8
IDX_DIM = 64
SB_WIDTH = SB_HEADS * HEAD_DIM
SB_KV_WIDTH = SB_KV_HEADS * HEAD_DIM
DSA_WIDTH = DSA_HEADS * HEAD_DIM
DSA_KV_WIDTH = DSA_KV_HEADS * HEAD_DIM
DSA_TOPK_MAX = 256
Q_BLOCK = 128
ROPE_THETA = 10000.0
NORM_EPS = 1e-6
ATTN_SCALE = HEAD_DIM ** -0.5
IDX_SCALE = (IDX_HEADS * IDX_DIM) ** -0.5
SPLIT_SIZES = (SB_WIDTH, SB_KV_WIDTH, SB_KV_WIDTH, SB_WIDTH,
               DSA_WIDTH, DSA_KV_WIDTH, DSA_KV_WIDTH, DSA_WIDTH,
               IDX_HEADS * IDX_DIM, IDX_DIM, IDX_HEADS,
               D_MODEL, D_MODEL)
PROJ_WIDTH = sum(SPLIT_SIZES)

kernel_name = "stickbreak_dsa_gated_hybrid_step"


def _split_points():
    pts, acc = [], 0
    for s in SPLIT_SIZES[:-1]:
        acc += s
        pts.append(acc)
    return pts


def _rmsnorm(x, g):
    xf = x.astype(jnp.float32)
    y = xf * lax.rsqrt(jnp.mean(xf * xf, axis=-1, keepdims=True) + NORM_EPS)
    return (y * g.astype(jnp.float32)).astype(x.dtype)


def _rope(x, pos):
    half = x.shape[-1] // 2
    inv = ROPE_THETA ** (-jnp.arange(half, dtype=jnp.float32) / half)
    ang = pos.astype(jnp.float32)[:, None] * inv[None, :]
    cos = jnp.cos(ang)[:, None, :]
    sin = jnp.sin(ang)[:, None, :]
    xf = x.astype(jnp.float32)
    x1, x2 = xf[..., :half], xf[..., half:]
    return jnp.concatenate([x1 * cos - x2 * sin, x2 * cos + x1 * sin], axis=-1).astype(x.dtype)


def _token_features(x, pos, norm_g, w_in, q_norm_g, k_norm_g, idx_k_norm_g):
    lead = x.shape[:2]
    xn = _rmsnorm(x, norm_g)
    proj = xn @ w_in
    (q_sb, k_sb, v_sb, z_sb, q_d, k_d, v_d, z_d,
     q_i, k_i, w_i, g_sb, g_d) = jnp.split(proj, _split_points(), axis=-1)
    g_grp_sb = SB_HEADS // SB_KV_HEADS
    g_grp_d = DSA_HEADS // DSA_KV_HEADS
    q_sb = q_sb.reshape(*lead, SB_KV_HEADS, g_grp_sb, HEAD_DIM)
    k_sb = k_sb.reshape(*lead, SB_KV_HEADS, HEAD_DIM)
    v_sb = v_sb.reshape(*lead, SB_KV_HEADS, HEAD_DIM)
    q_d = _rope(_rmsnorm(q_d.reshape(*lead, DSA_HEADS, HEAD_DIM), q_norm_g), pos)
    q_d = q_d.reshape(*lead, DSA_KV_HEADS, g_grp_d, HEAD_DIM)
    k_d = _rope(_rmsnorm(k_d.reshape(*lead, DSA_KV_HEADS, HEAD_DIM), k_norm_g), pos)
    v_d = v_d.reshape(*lead, DSA_KV_HEADS, HEAD_DIM)
    q_i = _rope(q_i.reshape(*lead, IDX_HEADS, IDX_DIM), pos)
    k_i = _rope(_rmsnorm(k_i, idx_k_norm_g)[..., None, :], pos)[..., 0, :]
    return (q_sb, k_sb, v_sb, z_sb, q_d, k_d, v_d, z_d, q_i, k_i, w_i, g_sb, g_d)


def _stick_breaking(q, k, v, qpos, kpos):
    z = jnp.einsum('btkgd,blkd->bkgtl', q, k, preferred_element_type=jnp.float32) * ATTN_SCALE
    allowed = kpos[None, :] < qpos[:, None]
    log_1m = jnp.where(allowed, jax.nn.log_sigmoid(-z), 0.0)
    tail = lax.cumsum(log_1m, axis=z.ndim - 1, reverse=True) - log_1m
    a = jnp.where(allowed, jnp.exp(jax.nn.log_sigmoid(z) + tail), 0.0)
    return jnp.einsum('bkgtl,blkd->btkgd', a.astype(v.dtype), v)


def _index_select(q_i, w_i, k_i, qpos, kpos, topk):
    s = jnp.einsum('bthd,bld->bthl', q_i, k_i, preferred_element_type=jnp.float32)
    score = jnp.einsum('bth,bthl->btl', w_i.astype(jnp.float32), jax.nn.relu(s)) * IDX_SCALE
    score = jnp.where(kpos[None, None, :] <= qpos[None, :, None], score, -jnp.inf)
    _, idx = lax.top_k(score, topk)
    return idx


def _sparse_attend(q, ksel, vsel, valid):
    logits = jnp.einsum('btkgd,btnkd->btkgn', q, ksel, preferred_element_type=jnp.float32) * ATTN_SCALE
    logits = jnp.where(valid[:, :, None, None, :], logits, -jnp.inf)
    p = jax.nn.softmax(logits, axis=-1)
    return jnp.einsum('btkgn,btnkd->btkgd', p.astype(vsel.dtype), vsel)


def _take_rows(a, idx):
    return jax.vmap(lambda ab, ib: ab[ib])(a, idx)


def _take_paged_rows(pool, new, page_table, idx):
    past = page_table.shape[1] * PAGE_SIZE
    pidx = jnp.minimum(idx, past - 1)
    phys = jax.vmap(lambda pt, ib: pt[ib // PAGE_SIZE])(page_table, pidx)
    rows_past = pool[phys, pidx % PAGE_SIZE].astype(new.dtype)
    rows_new = _take_rows(new, jnp.clip(idx - past, 0, new.shape[1] - 1))
    return jnp.where((idx < past)[..., None, None], rows_past, rows_new)


def _full_past(pool, page_table, new):
    db = page_table.shape[0]
    rows = pool[page_table].reshape(db, -1, *pool.shape[2:]).astype(new.dtype)
    return jnp.concatenate([rows, new], axis=1)


def _merge(x, o_sb, z_sb, o_d, z_d, g_sb, g_d, w_up_sb, w_up_dsa, w_out):
    lead = x.shape[:2]
    u_sb = (o_sb.reshape(*lead, SB_WIDTH).astype(x.dtype) * jax.nn.silu(z_sb)) @ w_up_sb
    u_d = (o_d.reshape(*lead, DSA_WIDTH).astype(x.dtype) * jax.nn.silu(z_d)) @ w_up_dsa
    mixed = jax.nn.sigmoid(g_sb) * u_sb + jax.nn.sigmoid(g_d) * u_d
    return x + mixed @ w_out


def _prompt_layer(x, norm_g, w_in, q_norm_g, k_norm_g, idx_k_norm_g, w_up_sb, w_up_dsa, w_out):
    b, s, _ = x.shape
    pos = jnp.arange(s)
    (q_sb, k_sb, v_sb, z_sb, q_d, k_d, v_d, z_d, q_i, k_i, w_i, g_sb, g_d) = _token_features(
        x, pos, norm_g, w_in, q_norm_g, k_norm_g, idx_k_norm_g)
    nb = s // Q_BLOCK
    topk = min(DSA_TOPK_MAX, s // 4)

    def blocks(a):
        return a.reshape(b, nb, Q_BLOCK, *a.shape[2:]).swapaxes(0, 1)

    def unblocks(a):
        return a.swapaxes(0, 1).reshape(b, s, *a.shape[3:])

    def step(args):
        qs, qd, qi, wi, qp = args
        o_sb = _stick_breaking(qs, k_sb, v_sb, qp, pos)
        idx = _index_select(qi, wi, k_i, qp, pos, topk)
        o_d = _sparse_attend(qd, _take_rows(k_d, idx), _take_rows(v_d, idx), idx <= qp[None, :, None])
        return o_sb, o_d

    o_sb, o_d = lax.map(step, (blocks(q_sb), blocks(q_d), blocks(q_i), blocks(w_i), pos.reshape(nb, Q_BLOCK)))
    y = _merge(x, unblocks(o_sb), z_sb, unblocks(o_d), z_d, g_sb, g_d, w_up_sb, w_up_dsa, w_out)
    return y, k_sb, v_sb, k_d, v_d, k_i


def _sample_layer(x, c_sb_k, c_sb_v, c_dsa_k, c_dsa_v, c_idx_k, page_table,
                  norm_g, w_in, q_norm_g, k_norm_g, idx_k_norm_g, w_up_sb, w_up_dsa, w_out):
    t = x.shape[1]
    past = page_table.shape[1] * PAGE_SIZE
    qpos = past + jnp.arange(t)
    kpos = jnp.arange(past + t)
    (q_sb, k_sb, v_sb, z_sb, q_d, k_d, v_d, z_d, q_i, k_i, w_i, g_sb, g_d) = _token_features(
        x, qpos, norm_g, w_in, q_norm_g, k_norm_g, idx_k_norm_g)
    o_sb = _stick_breaking(q_sb, _full_past(c_sb_k, page_table, k_sb),
                           _full_past(c_sb_v, page_table, v_sb), qpos, kpos)
    topk = min(DSA_TOPK_MAX, (past + t) // 4)
    idx = _index_select(q_i, w_i, _full_past(c_idx_k, page_table, k_i), qpos, kpos, topk)
    ksel = _take_paged_rows(c_dsa_k, k_d, page_table, idx)
    vsel = _take_paged_rows(c_dsa_v, v_d, page_table, idx)
    o_d = _sparse_attend(q_d, ksel, vsel, idx <= qpos[None, :, None])
    y = _merge(x, o_sb, z_sb, o_d, z_d, g_sb, g_d, w_up_sb, w_up_dsa, w_out)
    return y, k_sb, v_sb, k_d, v_d, k_i


def setup_inputs(seed: int = 0) -> dict:
    key = jax.random.key(seed)
    ks = jax.random.split(key, 20)
    n_pages = PAST_LEN // PAGE_SIZE
    n_pool = (DEC_BATCH * n_pages * 5) // 4
    nrm = jax.random.normal
    f32 = jnp.float32
    page_table = jax.random.permutation(ks[7], n_pool)[:DEC_BATCH * n_pages]
    page_table = page_table.reshape(DEC_BATCH, n_pages).astype(jnp.int32)
    return {
        "x_prompt": nrm(ks[0], (BATCH, SEQ, D_MODEL), f32),
        "x_sample": nrm(ks[1], (DEC_BATCH, DEC_SEQ, D_MODEL), f32),
        "cache_sb_k": nrm(ks[2], (DEPTH, n_pool, PAGE_SIZE, SB_KV_HEADS, HEAD_DIM), f32),
        "cache_sb_v": nrm(ks[3], (DEPTH, n_pool, PAGE_SIZE, SB_KV_HEADS, HEAD_DIM), f32),
        "cache_dsa_k": nrm(ks[4], (DEPTH, n_pool, PAGE_SIZE, DSA_KV_HEADS, HEAD_DIM), f32),
        "cache_dsa_v": nrm(ks[5], (DEPTH, n_pool, PAGE_SIZE, DSA_KV_HEADS, HEAD_DIM), f32),
        "cache_idx_k": nrm(ks[6], (DEPTH, n_pool, PAGE_SIZE, IDX_DIM), f32),
        "page_table": page_table,
        "norm_g": 1.0 + 0.02 * nrm(ks[8], (DEPTH, D_MODEL), f32),
        "w_in": nrm(ks[9], (DEPTH, D_MODEL, PROJ_WIDTH), f32) * D_MODEL ** -0.5,
        "q_norm_g": 1.0 + 0.02 * nrm(ks[10], (DEPTH, HEAD_DIM), f32),
        "k_norm_g": 1.0 + 0.02 * nrm(ks[11], (DEPTH, HEAD_DIM), f32),
        "idx_k_norm_g": 1.0 + 0.02 * nrm(ks[12], (DEPTH, IDX_DIM), f32),
        "w_up_sb": nrm(ks[13], (DEPTH, SB_WIDTH, D_MODEL), f32) * SB_WIDTH ** -0.5,
        "w_up_dsa": nrm(ks[14], (DEPTH, DSA_WIDTH, D_MODEL), f32) * DSA_WIDTH ** -0.5,
        "w_out": nrm(ks[15], (DEPTH, D_MODEL, D_MODEL), f32) * D_MODEL ** -0.5,
    }


def reference(x_prompt, x_sample, cache_sb_k, cache_sb_v, cache_dsa_k, cache_dsa_v, cache_idx_k,
              page_table, norm_g, w_in, q_norm_g, k_norm_g, idx_k_norm_g, w_up_sb, w_up_dsa, w_out):
    h_p, h_s = x_prompt, x_sample
    p_sbk, p_sbv, p_dk, p_dv, p_ik = [], [], [], [], []
    s_sbk, s_sbv, s_dk, s_dv, s_ik = [], [], [], [], []
    for l in range(DEPTH):
        h_p, a, b, c, d, e = _prompt_layer(h_p, norm_g[l], w_in[l], q_norm_g[l], k_norm_g[l],
                                           idx_k_norm_g[l], w_up_sb[l], w_up_dsa[l], w_out[l])
        p_sbk.append(a); p_sbv.append(b); p_dk.append(c); p_dv.append(d); p_ik.append(e)
        h_s, a, b, c, d, e = _sample_layer(h_s, cache_sb_k[l], cache_sb_v[l], cache_dsa_k[l],
                                           cache_dsa_v[l], cache_idx_k[l], page_table,
                                           norm_g[l], w_in[l], q_norm_g[l], k_norm_g[l],
                                           idx_k_norm_g[l], w_up_sb[l], w_up_dsa[l], w_out[l])
        s_sbk.append(a); s_sbv.append(b); s_dk.append(c); s_dv.append(d); s_ik.append(e)
    return (h_p, h_s,
            jnp.stack(p_sbk), jnp.stack(p_sbv), jnp.stack(p_dk), jnp.stack(p_dv), jnp.stack(p_ik),
            jnp.stack(s_sbk), jnp.stack(s_sbv), jnp.stack(s_dk), jnp.stack(s_dv), jnp.stack(s_ik))
```

```python
import functools

import jax
import jax.numpy as jnp
from jax import lax
from jax.experimental import pallas as pl
from jax.experimental.pallas import tpu as pltpu

F32 = jnp.float32
BF16 = jnp.bfloat16
I32 = jnp.int32

LANES = 128
HEAD_DIM = 64
HALF = HEAD_DIM // 2
SB_HEADS, SB_KV_HEADS = 8, 4
DSA_HEADS, DSA_KV_HEADS = 8, 2
IDX_HEADS, IDX_DIM = 8, 64
D_MODEL = 1024
PAGE_SIZE = 128
DSA_TOPK_MAX = 256
ROPE_THETA = 10000.0
NORM_EPS = 1e-6
ATTN_SCALE = HEAD_DIM ** -0.5
IDX_SCALE = (IDX_HEADS * IDX_DIM) ** -0.5
INT_MIN = -2 ** 31
NEG_BIG = -1e30
SB_DEAD = -110.0
VMEM_LIMIT = 56 * 1024 * 1024

_C_QSB, _C_KSB, _C_VSB, _C_ZSB = 0, 512, 768, 1024
_C_QD, _C_KD, _C_VD, _C_ZD = 1536, 2048, 2176, 2304
_C_QI, _C_KI, _C_GSB, _C_GD, _C_END = 2816, 3328, 3456, 4480, 5504
_W_SPLIT = 3400


def _dot(a, b):
    return jnp.dot(a, b, preferred_element_type=F32)


def _dot_nt(a, b):
    return lax.dot_general(a, b, (((1,), (1,)), ((), ())), preferred_element_type=F32)


def _dot_tn(a, b):
    return lax.dot_general(a, b, (((0,), (0,)), ((), ())), preferred_element_type=F32)


def _split_bf16(x):
    hi = x.astype(BF16)
    lo = (x - hi.astype(F32)).astype(BF16)
    return hi, lo


def _feat_kernel(x_ref, g_ref, w_ref, cos_ref, sin_ref, qn_ref, kn_ref, in_ref,
                 ksb_ref, vsb_ref, kd_ref, vd_ref, ki_ref,
                 qsb_ref, zsb_ref, qd_ref, zd_ref, qi_ref, kiw_ref, gsb_ref, gd_ref,
                 kesb_ref, vesb_ref, ked_ref, ved_ref, kei_ref):
    x = x_ref[...]
    ms = jnp.mean(x * x, axis=-1, keepdims=True)
    xn = (x * lax.rsqrt(ms + NORM_EPS) * g_ref[...]).astype(BF16)

    tm = x.shape[0]
    lane = lax.broadcasted_iota(I32, (tm, LANES), 1)
    lo_half = lane < HEAD_DIM
    lo_rot = (lane % HEAD_DIM) < HALF
    r = lax.broadcasted_iota(I32, (LANES, LANES), 0) // HEAD_DIM
    c = lax.broadcasted_iota(I32, (LANES, LANES), 1) // HEAD_DIM
    group_ones = (r == c).astype(F32)
    cos = cos_ref[...]
    sin = sin_ref[...]

    def proj(c0, c1):
        return _dot(xn, w_ref[:, c0:c1])

    def head_norm(v, gain):
        ss = jnp.dot(v * v, group_ones, precision=lax.Precision.HIGHEST, preferred_element_type=F32)
        return v * lax.rsqrt(ss * (1.0 / HEAD_DIM) + NORM_EPS) * gain

    def rope(v):
        rot = jnp.where(lo_rot, pltpu.roll(v, LANES - HALF, 1), pltpu.roll(v, HALF, 1))
        return v * cos + rot * sin

    def expand(pair, e):
        if e == 0:
            a = jnp.where(lo_half, pair, 0.0)
            return a, pltpu.roll(a, HEAD_DIM, 1)
        b = jnp.where(lo_half, 0.0, pair)
        return pltpu.roll(b, HEAD_DIM, 1), b

    def store_expanded(dst_ref, val, n_heads):
        for h in range(n_heads):
            pair = val[:, (h // 2) * LANES:(h // 2 + 1) * LANES]
            a, b = expand(pair, h % 2)
            dst_ref[:, h * 2 * LANES:h * 2 * LANES + LANES] = a.astype(BF16)
            dst_ref[:, h * 2 * LANES + LANES:(h + 1) * 2 * LANES] = b.astype(BF16)

    qsb_ref[...] = (proj(_C_QSB, _C_KSB) * ATTN_SCALE).astype(BF16)
    ksb = proj(_C_KSB, _C_VSB)
    ksb_ref[...] = ksb
    store_expanded(kesb_ref, ksb, SB_KV_HEADS)
    vsb = proj(_C_VSB, _C_ZSB)
    vsb_ref[...] = vsb
    store_expanded(vesb_ref, vsb, SB_KV_HEADS)
    zsb_ref[...] = proj(_C_ZSB, _C_QD)

    qd = proj(_C_QD, _C_KD)
    for p in range(DSA_HEADS // 2):
        blk = rope(head_norm(qd[:, p * LANES:(p + 1) * LANES], qn_ref[...]))
        qd_ref[:, p * LANES:(p + 1) * LANES] = (blk * ATTN_SCALE).astype(BF16)
    kd = rope(head_norm(proj(_C_KD, _C_VD), kn_ref[...]))
    kd_ref[...] = kd
    store_expanded(ked_ref, kd, DSA_KV_HEADS)
    vd = proj(_C_VD, _C_ZD)
    vd_ref[...] = vd
    store_expanded(ved_ref, vd, DSA_KV_HEADS)
    zd_ref[...] = proj(_C_ZD, _C_QI)

    qi = proj(_C_QI, _C_KI)
    for p in range(IDX_HEADS // 2):
        qi_ref[:, p * LANES:(p + 1) * LANES] = rope(qi[:, p * LANES:(p + 1) * LANES]).astype(BF16)
    kiw = proj(_C_KI, _C_GSB)
    kiw_ref[...] = kiw
    ki = rope(head_norm(kiw, in_ref[...]))
    ki_ref[...] = ki[:, :IDX_DIM]
    kei_ref[:, :LANES] = ki.astype(BF16)
    kei_ref[:, LANES:] = pltpu.roll(ki, HEAD_DIM, 1).astype(BF16)

    gsb_ref[...] = proj(_C_GSB, _C_GD)
    gd_ref[...] = proj(_C_GD, _C_END)


def _features(x2d, rows_per_pos_cycle, cos, sin, norm_g, w_r, qn, kn, inn, tm):
    R = x2d.shape[0]
    n_pos_blocks = rows_per_pos_cycle // tm
    row = lambda i: (i, 0)
    const = lambda i: (0, 0)
    pos = lambda i: (i % n_pos_blocks, 0)
    f32_widths = dict(ksb=256, vsb=256, kd=128, vd=128, ki=64)
    names = ["ksb", "vsb", "kd", "vd", "ki", "qsb", "zsb", "qd", "zd", "qi", "kiw", "gsb", "gd",
             "kesb", "vesb", "ked", "ved", "kei"]
    widths = dict(f32_widths, qsb=512, zsb=512, qd=512, zd=512, qi=512, kiw=128, gsb=1024, gd=1024,
                  kesb=1024, vesb=1024, ked=512, ved=512, kei=256)
    dtypes = {n: F32 for n in names}
    for n in ("qsb", "qd", "qi", "kesb", "vesb", "ked", "ved", "kei"):
        dtypes[n] = BF16
    outs = pl.pallas_call(
        _feat_kernel,
        grid=(R // tm,),
        in_specs=[
            pl.BlockSpec((tm, D_MODEL), row),
            pl.BlockSpec((1, D_MODEL), const),
            pl.BlockSpec((D_MODEL, _C_END), const),
            pl.BlockSpec((tm, LANES), pos),
            pl.BlockSpec((tm, LANES), pos),
            pl.BlockSpec((1, LANES), const),
            pl.BlockSpec((1, LANES), const),
            pl.BlockSpec((1, LANES), const),
        ],
        out_specs=[pl.BlockSpec((tm, widths[n]), row) for n in names],
        out_shape=[jax.ShapeDtypeStruct((R, widths[n]), dtypes[n]) for n in names],
        compiler_params=pltpu.CompilerParams(dimension_semantics=("arbitrary",),
                                             vmem_limit_bytes=VMEM_LIMIT),
        name="features",
    )(x2d, norm_g, w_r, cos, sin, qn, kn, inn)
    return dict(zip(names, outs))


def _sb_softplus_parts(z):
    e = jnp.exp(-jnp.abs(z))
    return jnp.maximum(z, 0.0) + jnp.log(1.0 + e)


def _sb_kernel(q_ref, ke_ref, ve_ref, o_ref, carry_ref, acc_ref, *, T):
    i = pl.program_id(1)
    rr = lax.broadcasted_iota(I32, (T, T), 0)
    cc = lax.broadcasted_iota(I32, (T, T), 1)
    suffix = (rr > cc).astype(BF16)
    ones = jnp.ones((T, LANES), BF16)
    allowed1 = cc < rr
    allowed = jnp.concatenate([allowed1, allowed1], axis=0)

    def widen(v):
        return v if T == LANES else jnp.concatenate([v] * (T // LANES), axis=1)

    for kvh in range(SB_KV_HEADS):
        q2 = q_ref[:, kvh * LANES:(kvh + 1) * LANES]
        c0 = kvh * 2 * LANES

        def block(j, diag):
            start = pl.multiple_of(j * T, T)
            ka = ke_ref[pl.ds(start, T), c0:c0 + LANES]
            kb = ke_ref[pl.ds(start, T), c0 + LANES:c0 + 2 * LANES]
            z = jnp.concatenate([_dot_nt(q2, ka), _dot_nt(q2, kb)], axis=0)
            l1m = -_sb_softplus_parts(z)
            if diag:
                l1m = jnp.where(allowed, l1m, 0.0)
            hi, lo = _split_bf16(l1m)
            tail = _dot(hi, suffix) + _dot(lo, suffix)
            tot = _dot(hi, ones) + _dot(lo, ones)
            if diag:
                logit = z + l1m + tail
                a = jnp.where(allowed, jnp.exp(logit), 0.0).astype(BF16)
            else:
                logit = z + l1m + tail + widen(carry_ref[...])
                a = jnp.exp(logit).astype(BF16)
            va = ve_ref[pl.ds(start, T), c0:c0 + LANES]
            vb = ve_ref[pl.ds(start, T), c0 + LANES:c0 + 2 * LANES]
            o = _dot(a[:T], va) + _dot(a[T:], vb)
            if diag:
                carry_ref[...] = tot
                acc_ref[...] = o
            else:
                carry_ref[...] = carry_ref[...] + tot
                acc_ref[...] = acc_ref[...] + o
            return jnp.max(carry_ref[...])

        live = block(i, True)

        def cond(s):
            return jnp.logical_and(s[0] >= 0, s[1] >= SB_DEAD)

        def body(s):
            return s[0] - 1, block(s[0], False)

        lax.while_loop(cond, body, (i - 1, live))
        o_ref[:, kvh * LANES:(kvh + 1) * LANES] = acc_ref[...]


def _sb_attention(qsb, kesb, vesb, B, S, T):
    nq = S // T
    return pl.pallas_call(
        functools.partial(_sb_kernel, T=T),
        grid=(B, nq),
        in_specs=[
            pl.BlockSpec((T, 512), lambda b, i: (b * nq + i, 0)),
            pl.BlockSpec((S, 1024), lambda b, i: (b, 0)),
            pl.BlockSpec((S, 1024), lambda b, i: (b, 0)),
        ],
        out_specs=pl.BlockSpec((T, 512), lambda b, i: (b * nq + i, 0)),
        out_shape=jax.ShapeDtypeStruct((B * S, 512), F32),
        scratch_shapes=[pltpu.VMEM((2 * T, LANES), F32), pltpu.VMEM((T, LANES), F32)],
        compiler_params=pltpu.CompilerParams(dimension_semantics=("arbitrary", "arbitrary"),
                                             vmem_limit_bytes=VMEM_LIMIT),
        name="sb_prompt",
    )(qsb, kesb, vesb)


def _sort_key(score):
    score = jnp.where(score == 0.0, 0.0, score)
    bits = pltpu.bitcast(score, I32)
    return bits ^ ((bits >> 31) & 0x7FFFFFFF)


def _count_rows(key_ref, n_chunks, chunk, pred):
    width = key_ref.shape[1]

    def body(j, acc):
        start = pl.multiple_of(j * chunk, chunk)
        keys = key_ref[pl.ds(start, chunk), :]
        idx = start + lax.broadcasted_iota(I32, (chunk, width), 0)
        hit = jnp.where(pred(keys, idx), 1, 0).astype(I32)
        return acc + hit.reshape(chunk // 8, 8, width).sum(axis=0)

    acc = lax.fori_loop(0, n_chunks, body, jnp.zeros((8, width), I32))
    return acc.sum(axis=0, keepdims=True)


def _select_threshold(key_ref, n_chunks, chunk, topk, idx_bits):
    width = key_ref.shape[1]

    def bit_step(k, t):
        bit = jnp.left_shift(jnp.int32(1), 31 - k)
        cand = t ^ bit
        cnt = _count_rows(key_ref, n_chunks, chunk, lambda keys, idx: keys >= cand)
        return jnp.where(cnt >= topk, cand, t)

    t = lax.fori_loop(0, 32, bit_step, jnp.full((1, width), INT_MIN, I32))
    n_gt = _count_rows(key_ref, n_chunks, chunk, lambda keys, idx: keys > t)
    n_ge = _count_rows(key_ref, n_chunks, chunk, lambda keys, idx: keys >= t)
    need = topk - n_gt
    all_rows = jnp.int32(2 ** idx_bits)
    open_cidx = jnp.where(t == INT_MIN, -1, all_rows)
    tied = jnp.logical_and(n_ge > topk, t != INT_MIN)

    def tie_break(_):
        def idx_step(k, c):
            bit = jnp.left_shift(jnp.int32(1), idx_bits - 1 - k)
            cand = c + bit
            cnt = _count_rows(key_ref, n_chunks, chunk,
                              lambda keys, idx: jnp.logical_and(keys == t, idx < cand))
            return jnp.where(cnt < need, cand, c)

        c = lax.fori_loop(0, idx_bits, idx_step, jnp.zeros((1, width), I32))
        return jnp.where(tied, c, open_cidx)

    any_tied = jnp.max(jnp.where(tied, 1, 0)) > 0
    cidx = lax.cond(any_tied, tie_break, lambda _: open_cidx, 0)
    return t, cidx


def _dsa_kernel(qi_ref, kiw_ref, kei_ref, qd_ref, ked_ref, ved_ref, o_ref,
                key_ref, m_ref, l_ref, acc_ref, *, T, topk, idx_bits):
    i = pl.program_id(1)
    n_chunks = i + 1
    rr = lax.broadcasted_iota(I32, (T, T), 0)
    cc = lax.broadcasted_iota(I32, (T, T), 1)
    w_t = kiw_ref[...].T[IDX_DIM:IDX_DIM + IDX_HEADS, :]

    def score_chunk(j, diag):
        start = pl.multiple_of(j * T, T)
        ka = kei_ref[pl.ds(start, T), :LANES]
        kb = kei_ref[pl.ds(start, T), LANES:]
        sc = jnp.zeros((T, T), F32)
        for p in range(IDX_HEADS // 2):
            qp = qi_ref[:, p * LANES:(p + 1) * LANES]
            sc = sc + w_t[2 * p:2 * p + 1, :] * jnp.maximum(_dot_nt(ka, qp), 0.0)
            sc = sc + w_t[2 * p + 1:2 * p + 2, :] * jnp.maximum(_dot_nt(kb, qp), 0.0)
        keys = _sort_key(sc * IDX_SCALE)
        if diag:
            keys = jnp.where(rr <= cc, keys, INT_MIN)
        key_ref[pl.ds(start, T), :] = keys

    def score_body(j, carry):
        score_chunk(j, False)
        return carry

    lax.fori_loop(0, i, score_body, 0)
    score_chunk(i, True)

    t, cidx = _select_threshold(key_ref, n_chunks, T, topk, idx_bits)

    m_ref[...] = jnp.full(m_ref.shape, NEG_BIG, F32)
    l_ref[...] = jnp.zeros(l_ref.shape, F32)
    acc_ref[...] = jnp.zeros(acc_ref.shape, F32)
    top_rows = lax.broadcasted_iota(I32, (LANES, T), 0) < HEAD_DIM

    def attend(j, carry):
        start = pl.multiple_of(j * T, T)
        keys = key_ref[pl.ds(start, T), :]
        idx = start + rr
        sel = jnp.logical_or(keys > t, jnp.logical_and(keys == t, idx <= cidx))
        for p in range(DSA_HEADS // 2):
            kvh = p // 2
            qp = qd_ref[:, p * LANES:(p + 1) * LANES]
            kcol = kvh * 2 * LANES
            probs = []
            alphas = []
            for e in range(2):
                h = 2 * p + e
                kblk = ked_ref[pl.ds(start, T), kcol + e * LANES:kcol + (e + 1) * LANES]
                lg = jnp.where(sel, _dot_nt(kblk, qp), NEG_BIG)
                m_old = m_ref[h:h + 1, :]
                m_new = jnp.maximum(m_old, jnp.max(lg, axis=0, keepdims=True))
                alpha = jnp.exp(m_old - m_new)
                pr = jnp.where(sel, jnp.exp(lg - m_new), 0.0)
                l_ref[h:h + 1, :] = alpha * l_ref[h:h + 1, :] + jnp.sum(pr, axis=0, keepdims=True)
                m_ref[h:h + 1, :] = m_new
                probs.append(pr.astype(BF16))
                alphas.append(alpha)
            va = ved_ref[pl.ds(start, T), kcol:kcol + LANES]
            vb = ved_ref[pl.ds(start, T), kcol + LANES:kcol + 2 * LANES]
            scale = jnp.where(top_rows, alphas[0], alphas[1])
            acc_ref[p] = acc_ref[p] * scale + _dot_tn(va, probs[0]) + _dot_tn(vb, probs[1])
        return carry

    lax.fori_loop(0, n_chunks, attend, 0)
    for p in range(DSA_HEADS // 2):
        denom = jnp.where(top_rows, l_ref[2 * p:2 * p + 1, :], l_ref[2 * p + 1:2 * p + 2, :])
        o_ref[:, p * LANES:(p + 1) * LANES] = (acc_ref[p] / denom).T


def _dsa_attention(qi, kiw, kei, qd, ked, ved, B, S, T, topk):
    nq = S // T
    idx_bits = max(1, (S - 1).bit_length())
    qrow = lambda b, i: (b * nq + i, 0)
    batch = lambda b, i: (b, 0)
    return pl.pallas_call(
        functools.partial(_dsa_kernel, T=T, topk=topk, idx_bits=idx_bits),
        grid=(B, nq),
        in_specs=[
            pl.BlockSpec((T, 512), qrow),
            pl.BlockSpec((T, LANES), qrow),
            pl.BlockSpec((S, 256), batch),
            pl.BlockSpec((T, 512), qrow),
            pl.BlockSpec((S, 512), batch),
            pl.BlockSpec((S, 512), batch),
        ],
        out_specs=pl.BlockSpec((T, 512), qrow),
        out_shape=jax.ShapeDtypeStruct((B * S, 512), F32),
        scratch_shapes=[
            pltpu.VMEM((S, T), I32),
            pltpu.VMEM((DSA_HEADS, T), F32),
            pltpu.VMEM((DSA_HEADS, T), F32),
            pltpu.VMEM((DSA_HEADS // 2, LANES, T), F32),
        ],
        compiler_params=pltpu.CompilerParams(dimension_semantics=("arbitrary", "arbitrary"),
                                             vmem_limit_bytes=VMEM_LIMIT),
        name="dsa_prompt",
    )(qi, kiw, kei, qd, ked, ved)


def _merge_kernel(x_ref, osb_ref, zsb_ref, od_ref, zd_ref, gsb_ref, gd_ref,
                  wsb_ref, wd_ref, wo_ref, y_ref):
    zsb = zsb_ref[...]
    zd = zd_ref[...]
    a_sb = (osb_ref[...] * (zsb * jax.nn.sigmoid(zsb))).astype(BF16)
    a_d = (od_ref[...] * (zd * jax.nn.sigmoid(zd))).astype(BF16)
    u_sb = _dot(a_sb, wsb_ref[...])
    u_d = _dot(a_d, wd_ref[...])
    mixed = jax.nn.sigmoid(gsb_ref[...]) * u_sb + jax.nn.sigmoid(gd_ref[...]) * u_d
    y_ref[...] = x_ref[...] + _dot(mixed.astype(BF16), wo_ref[...])


def _merge(x2d, osb, zsb, od, zd, gsb, gd, wsb, wd, wo, tm):
    R = x2d.shape[0]
    row = lambda i: (i, 0)
    const = lambda i: (0, 0)
    return pl.pallas_call(
        _merge_kernel,
        grid=(R // tm,),
        in_specs=[
            pl.BlockSpec((tm, D_MODEL), row),
            pl.BlockSpec((tm, 512), row),
            pl.BlockSpec((tm, 512), row),
            pl.BlockSpec((tm, 512), row),
            pl.BlockSpec((tm, 512), row),
            pl.BlockSpec((tm, D_MODEL), row),
            pl.BlockSpec((tm, D_MODEL), row),
            pl.BlockSpec((512, D_MODEL), const),
            pl.BlockSpec((512, D_MODEL), const),
            pl.BlockSpec((D_MODEL, D_MODEL), const),
        ],
        out_specs=pl.BlockSpec((tm, D_MODEL), row),
        out_shape=jax.ShapeDtypeStruct((R, D_MODEL), F32),
        compiler_params=pltpu.CompilerParams(dimension_semantics=("arbitrary",),
                                             vmem_limit_bytes=VMEM_LIMIT),
        name="merge",
    )(x2d, osb, zsb, od, zd, gsb, gd, wsb, wd, wo)


def _rope_tables(pos):
    inv = ROPE_THETA ** (-jnp.arange(HALF, dtype=F32) / HALF)
    ang = pos.astype(F32)[:, None] * inv[None, :]
    cos = jnp.cos(ang)
    sin = jnp.sin(ang)
    return jnp.tile(cos, (1, 4)), jnp.tile(jnp.concatenate([-sin, sin], axis=1), (1, 2))


def _prep_weights(norm_g, w_in, q_norm_g, k_norm_g, idx_k_norm_g, w_up_sb, w_up_dsa, w_out):
    pad = jnp.zeros((D_MODEL, _C_GSB - _C_KI - (_W_SPLIT - _C_KI)), w_in.dtype)
    w_r = jnp.concatenate([w_in[:, :_W_SPLIT], pad, w_in[:, _W_SPLIT:]], axis=1).astype(BF16)
    qn = jnp.tile(q_norm_g, 2)[None, :]
    kn = jnp.tile(k_norm_g, 2)[None, :]
    inn = jnp.concatenate([idx_k_norm_g, jnp.zeros((LANES - IDX_DIM,), F32)])[None, :]
    return (norm_g[None, :], w_r, qn, kn, inn,
            w_up_sb.astype(BF16), w_up_dsa.astype(BF16), w_out.astype(BF16))


def _prompt_layer(x, weights, t_feat, t_attn):
    B, S, _ = x.shape
    norm_g, w_r, qn, kn, inn, wsb, wd, wo = weights
    x2d = x.reshape(B * S, D_MODEL)
    cos, sin = _rope_tables(jnp.arange(S))
    f = _features(x2d, S, cos, sin, norm_g, w_r, qn, kn, inn, t_feat)
    topk = min(DSA_TOPK_MAX, S // 4)
    osb = _sb_attention(f["qsb"], f["kesb"], f["vesb"], B, S, t_attn)
    od = _dsa_attention(f["qi"], f["kiw"], f["kei"], f["qd"], f["ked"], f["ved"], B, S, LANES, topk)
    y = _merge(x2d, osb, f["zsb"], od, f["zd"], f["gsb"], f["gd"], wsb, wd, wo, t_feat)
    return (y.reshape(B, S, D_MODEL),
            f["ksb"].reshape(1, B, S, SB_KV_HEADS, HEAD_DIM),
            f["vsb"].reshape(1, B, S, SB_KV_HEADS, HEAD_DIM),
            f["kd"].reshape(1, B, S, DSA_KV_HEADS, HEAD_DIM),
            f["vd"].reshape(1, B, S, DSA_KV_HEADS, HEAD_DIM),
            f["ki"].reshape(1, B, S, IDX_DIM))


def _page_fetch(cache_hbm, pt_ref, buf, sem, b, slot, n_pages):
    def body(p, carry):
        pltpu.make_async_copy(cache_hbm.at[pt_ref[b, p]], buf.at[slot, p], sem.at[slot]).start()
        return carry
    lax.fori_loop(0, n_pages, body, 0)


def _page_wait(cache_hbm, buf, sem, slot, n_pages):
    def body(p, carry):
        pltpu.make_async_copy(cache_hbm.at[0], buf.at[slot, p], sem.at[slot]).wait()
        return carry
    lax.fori_loop(0, n_pages, body, 0)


def _stream_pages(caches, pt_ref, bufs, sems, n_pages):
    b = pl.program_id(0)
    nb = pl.num_programs(0)
    slot = b % 2

    @pl.when(b == 0)
    def _():
        for cache, buf, sem in zip(caches, bufs, sems):
            _page_fetch(cache, pt_ref, buf, sem, 0, 0, n_pages)

    @pl.when(b + 1 < nb)
    def _():
        for cache, buf, sem in zip(caches, bufs, sems):
            _page_fetch(cache, pt_ref, buf, sem, b + 1, 1 - slot, n_pages)

    for cache, buf, sem in zip(caches, bufs, sems):
        _page_wait(cache, buf, sem, slot, n_pages)
    return slot


def _dec_sb_kernel(pt_ref, qm_ref, k_hbm, v_hbm, o_ref, kbuf, vbuf, zbuf, sems, *, n_pages):
    slot = _stream_pages((k_hbm, v_hbm), pt_ref, (kbuf, vbuf), (sems.at[0], sems.at[1]), n_pages)
    qm = qm_ref[0]
    P = PAGE_SIZE
    rr = lax.broadcasted_iota(I32, (P, P), 0)
    cc = lax.broadcasted_iota(I32, (P, P), 1)
    suffix = (rr > cc).astype(BF16)
    ones = jnp.ones((P, LANES), BF16)

    def scores(p, carry):
        zbuf[p] = _dot(qm, kbuf[slot, p].astype(BF16))
        return carry

    lax.fori_loop(0, n_pages, scores, 0)

    def attend(k, state):
        carry, acc = state
        p = n_pages - 1 - k
        z = zbuf[p]
        l1m = -_sb_softplus_parts(z)
        hi, lo = _split_bf16(l1m)
        tail = _dot(hi, suffix) + _dot(lo, suffix)
        tot = _dot(hi, ones) + _dot(lo, ones)
        a = jnp.exp(z + l1m + tail + carry).astype(BF16)
        acc = acc + _dot_nt(a, vbuf[slot, p].astype(BF16))
        return carry + tot, acc

    n_heads = qm.shape[0]
    _, acc = lax.fori_loop(0, n_pages, attend,
                           (jnp.zeros((n_heads, LANES), F32), jnp.zeros((n_heads, 2 * LANES), F32)))
    o_ref[0] = acc


def _dec_sb(page_table, qm, cache_k, cache_v):
    DB, n_pages = page_table.shape
    width = cache_k.shape[1]
    return pl.pallas_call(
        functools.partial(_dec_sb_kernel, n_pages=n_pages),
        grid_spec=pltpu.PrefetchScalarGridSpec(
            num_scalar_prefetch=1,
            grid=(DB,),
            in_specs=[
                pl.BlockSpec((1, SB_HEADS, width), lambda b, pt: (b, 0, 0)),
                pl.BlockSpec(memory_space=pl.ANY),
                pl.BlockSpec(memory_space=pl.ANY),
            ],
            out_specs=pl.BlockSpec((1, SB_HEADS, width), lambda b, pt: (b, 0, 0)),
            scratch_shapes=[
                pltpu.VMEM((2, n_pages, width, PAGE_SIZE), F32),
                pltpu.VMEM((2, n_pages, width, PAGE_SIZE), F32),
                pltpu.VMEM((n_pages, SB_HEADS, PAGE_SIZE), F32),
                pltpu.SemaphoreType.DMA((2, 2)),
            ]),
        out_shape=jax.ShapeDtypeStruct((DB, SB_HEADS, width), F32),
        compiler_params=pltpu.CompilerParams(dimension_semantics=("arbitrary",),
                                             vmem_limit_bytes=VMEM_LIMIT),
        name="dec_sb",
    )(page_table, qm, cache_k, cache_v)


def _dec_idx_kernel(pt_ref, qi_ref, w_ref, kinew_ref, k_hbm, key_ref, kbuf, sems, *, n_pages, n_rows):
    slot = _stream_pages((k_hbm,), pt_ref, (kbuf,), (sems.at[0],), n_pages)
    qi = qi_ref[0]
    w = w_ref[0]

    def score(p, carry):
        s = _dot(qi, kbuf[slot, p].astype(BF16))
        sc = jnp.sum(w * jnp.maximum(s, 0.0), axis=0, keepdims=True) * IDX_SCALE
        key_ref[0, pl.ds(p, 1), :] = _sort_key(sc)
        return carry

    lax.fori_loop(0, n_pages, score, 0)
    s_new = jnp.sum(qi.astype(F32) * kinew_ref[0].astype(F32), axis=1, keepdims=True)
    sc_new = jnp.sum(w * jnp.maximum(s_new, 0.0), axis=0, keepdims=True) * IDX_SCALE
    tail_rows = n_rows - n_pages
    r = lax.broadcasted_iota(I32, (tail_rows, LANES), 0)
    c = lax.broadcasted_iota(I32, (tail_rows, LANES), 1)
    key_new = jnp.broadcast_to(_sort_key(sc_new), (tail_rows, LANES))
    key_ref[0, n_pages:, :] = jnp.where(jnp.logical_and(r == 0, c == 0), key_new, INT_MIN)


def _dec_idx(page_table, qi, w, kinew, cache_k, n_rows):
    DB, n_pages = page_table.shape
    per_seq = lambda b, pt: (b, 0, 0)
    return pl.pallas_call(
        functools.partial(_dec_idx_kernel, n_pages=n_pages, n_rows=n_rows),
        grid_spec=pltpu.PrefetchScalarGridSpec(
            num_scalar_prefetch=1,
            grid=(DB,),
            in_specs=[
                pl.BlockSpec((1, IDX_HEADS, IDX_DIM), per_seq),
                pl.BlockSpec((1, IDX_HEADS, 1), per_seq),
                pl.BlockSpec((1, 1, IDX_DIM), per_seq),
                pl.BlockSpec(memory_space=pl.ANY),
            ],
            out_specs=pl.BlockSpec((1, n_rows, LANES), per_seq),
            scratch_shapes=[
                pltpu.VMEM((2, n_pages, IDX_DIM, PAGE_SIZE), F32),
                pltpu.SemaphoreType.DMA((1, 2)),
            ]),
        out_shape=jax.ShapeDtypeStruct((DB, n_rows, LANES), I32),
        compiler_params=pltpu.CompilerParams(dimension_semantics=("arbitrary",),
                                             vmem_limit_bytes=VMEM_LIMIT),
        name="dec_idx",
    )(page_table, qi, w, kinew, cache_k)


def _dec_thr_kernel(keys_ref, t_ref, c_ref, key_scr, *, n_rows, topk, idx_bits):
    for p in range(n_rows):
        blk = pltpu.bitcast(keys_ref[:, p * LANES:(p + 1) * LANES], F32)
        key_scr[p * LANES:(p + 1) * LANES, :] = pltpu.bitcast(blk.T, I32)
    t, cidx = _select_threshold(key_scr, n_rows, LANES, topk, idx_bits)
    t_ref[...] = t
    c_ref[...] = cidx


def _dec_thr(keys2d, n_rows, topk):
    DB = keys2d.shape[0]
    idx_bits = (n_rows * LANES - 1).bit_length()
    return pl.pallas_call(
        functools.partial(_dec_thr_kernel, n_rows=n_rows, topk=topk, idx_bits=idx_bits),
        out_shape=[jax.ShapeDtypeStruct((1, DB), I32), jax.ShapeDtypeStruct((1, DB), I32)],
        scratch_shapes=[pltpu.VMEM((n_rows * LANES, DB), I32)],
        compiler_params=pltpu.CompilerParams(vmem_limit_bytes=VMEM_LIMIT),
        name="dec_thr",
    )(keys2d)


def _dec_dsa_kernel(pt_ref, t_ref, c_ref, qm_ref, keys_ref, kdnew_ref, vdnew_ref, k_hbm, v_hbm, o_ref,
                    kbuf, vbuf, lbuf, sems, *, n_pages):
    b = pl.program_id(0)
    slot = _stream_pages((k_hbm, v_hbm), pt_ref, (kbuf, vbuf), (sems.at[0], sems.at[1]), n_pages)
    qm = qm_ref[0]
    t = t_ref[b]
    cidx = c_ref[b]
    lane = lax.broadcasted_iota(I32, (1, LANES), 1)

    def selected(p):
        keys = keys_ref[0, pl.ds(p, 1), :]
        idx = p * PAGE_SIZE + lane
        return jnp.logical_or(keys > t, jnp.logical_and(keys == t, idx <= cidx))

    def logits(p, m):
        lg = jnp.where(selected(p), _dot(qm, kbuf[slot, p].astype(BF16)), NEG_BIG)
        lbuf[p] = lg
        return jnp.maximum(m, lg)

    n_heads = qm.shape[0]
    m = lax.fori_loop(0, n_pages, logits, jnp.full((n_heads, LANES), NEG_BIG, F32))
    sel_new = selected(n_pages)[:, :1]
    lg_new = jnp.sum(qm.astype(F32) * kdnew_ref[0].astype(F32), axis=1, keepdims=True)
    lg_new = jnp.where(sel_new, lg_new, NEG_BIG)
    m = jnp.maximum(jnp.max(m, axis=1, keepdims=True), lg_new)

    def attend(p, state):
        l, acc = state
        pr = jnp.where(selected(p), jnp.exp(lbuf[p] - m), 0.0)
        return l + pr, acc + _dot_nt(pr.astype(BF16), vbuf[slot, p].astype(BF16))

    l, acc = lax.fori_loop(0, n_pages, attend,
                           (jnp.zeros((n_heads, LANES), F32), jnp.zeros((n_heads, LANES), F32)))
    pr_new = jnp.where(sel_new, jnp.exp(lg_new - m), 0.0)
    pr_new_b = pr_new.astype(BF16).astype(F32)
    acc = acc + pr_new_b * vdnew_ref[0].astype(F32)
    denom = jnp.sum(l, axis=1, keepdims=True) + pr_new
    o_ref[0] = acc / denom


def _dec_dsa(page_table, t, cidx, qm, keys, kdnew, vdnew, cache_k, cache_v):
    DB, n_pages = page_table.shape
    n_rows = keys.shape[1]
    per_seq = lambda b, pt, t, c: (b, 0, 0)
    return pl.pallas_call(
        functools.partial(_dec_dsa_kernel, n_pages=n_pages),
        grid_spec=pltpu.PrefetchScalarGridSpec(
            num_scalar_prefetch=3,
            grid=(DB,),
            in_specs=[
                pl.BlockSpec((1, DSA_HEADS, LANES), per_seq),
                pl.BlockSpec((1, n_rows, LANES), per_seq),
                pl.BlockSpec((1, 1, LANES), per_seq),
                pl.BlockSpec((1, 1, LANES), per_seq),
                pl.BlockSpec(memory_space=pl.ANY),
                pl.BlockSpec(memory_space=pl.ANY),
            ],
            out_specs=pl.BlockSpec((1, DSA_HEADS, LANES), per_seq),
            scratch_shapes=[
                pltpu.VMEM((2, n_pages, PAGE_SIZE, LANES), F32),
                pltpu.VMEM((2, n_pages, PAGE_SIZE, LANES), F32),
                pltpu.VMEM((n_pages, DSA_HEADS, LANES), F32),
                pltpu.SemaphoreType.DMA((2, 2)),
            ]),
        out_shape=jax.ShapeDtypeStruct((DB, DSA_HEADS, LANES), F32),
        compiler_params=pltpu.CompilerParams(dimension_semantics=("arbitrary",),
                                             vmem_limit_bytes=VMEM_LIMIT),
        name="dec_dsa",
    )(page_table, t, cidx, qm, keys, kdnew, vdnew, cache_k, cache_v)


def _heads_in_kv_lanes(q, n_heads, n_kv):
    R = q.shape[0]
    q3 = q.reshape(R, n_heads, 1, HEAD_DIM)
    kv_of_head = jnp.arange(n_heads) // (n_heads // n_kv)
    onehot = (kv_of_head[:, None] == jnp.arange(n_kv)[None, :])[None, :, :, None]
    return jnp.where(onehot, q3, jnp.zeros((), q.dtype)).reshape(R, n_heads, n_kv * HEAD_DIM)


def _own_kv_lanes(o, n_heads, n_kv):
    R = o.shape[0]
    o4 = o.reshape(R, n_heads, n_kv, HEAD_DIM)
    group = n_heads // n_kv
    return jnp.concatenate([o4[:, h, h // group, :] for h in range(n_heads)], axis=1)


def _sample_layer(x, caches, page_table, weights):
    DB, T, _ = x.shape
    c_sb_k, c_sb_v, c_dsa_k, c_dsa_v, c_idx_k = caches
    norm_g, w_r, qn, kn, inn, wsb, wd, wo = weights
    n_pages = page_table.shape[1]
    past = n_pages * PAGE_SIZE
    x2d = x.reshape(DB * T, D_MODEL)
    cos, sin = _rope_tables(jnp.full((DB * T,), past))
    f = _features(x2d, DB * T, cos, sin, norm_g, w_r, qn, kn, inn, DB * T)

    n_pool = c_sb_k.shape[0]

    def pages_t(c):
        return jnp.moveaxis(c, 1, -1).reshape(n_pool, -1, PAGE_SIZE)

    qm_sb = _heads_in_kv_lanes(f["qsb"], SB_HEADS, SB_KV_HEADS)
    o_sb = _dec_sb(page_table, qm_sb, pages_t(c_sb_k), pages_t(c_sb_v))
    osb = _own_kv_lanes(o_sb, SB_HEADS, SB_KV_HEADS)

    n_rows = n_pages + 8
    qi3 = f["qi"].reshape(DB, IDX_HEADS, IDX_DIM)
    w3 = f["kiw"][:, IDX_DIM:IDX_DIM + IDX_HEADS].reshape(DB, IDX_HEADS, 1)
    kinew = f["kei"][:, :IDX_DIM].reshape(DB, 1, IDX_DIM)
    keys = _dec_idx(page_table, qi3, w3, kinew, pages_t(c_idx_k), n_rows)
    topk = min(DSA_TOPK_MAX, (past + T) // 4)
    t, cidx = _dec_thr(keys.reshape(DB, n_rows * LANES), n_rows, topk)

    qm_d = _heads_in_kv_lanes(f["qd"], DSA_HEADS, DSA_KV_HEADS)
    kdnew = f["kd"].astype(BF16).reshape(DB, 1, LANES)
    vdnew = f["vd"].astype(BF16).reshape(DB, 1, LANES)
    o_d = _dec_dsa(page_table, t.reshape(DB), cidx.reshape(DB), qm_d, keys, kdnew, vdnew,
                   pages_t(c_dsa_k), pages_t(c_dsa_v))
    od = _own_kv_lanes(o_d, DSA_HEADS, DSA_KV_HEADS)

    y = _merge(x2d, osb, f["zsb"], od, f["zd"], f["gsb"], f["gd"], wsb, wd, wo, DB * T)
    return (y.reshape(DB, T, D_MODEL),
            f["ksb"].reshape(1, DB, T, SB_KV_HEADS, HEAD_DIM),
            f["vsb"].reshape(1, DB, T, SB_KV_HEADS, HEAD_DIM),
            f["kd"].reshape(1, DB, T, DSA_KV_HEADS, HEAD_DIM),
            f["vd"].reshape(1, DB, T, DSA_KV_HEADS, HEAD_DIM),
            f["ki"].reshape(1, DB, T, IDX_DIM))


def kernel(x_prompt, x_sample, cache_sb_k, cache_sb_v, cache_dsa_k, cache_dsa_v, cache_idx_k, page_table,
           norm_g, w_in, q_norm_g, k_norm_g, idx_k_norm_g, w_up_sb, w_up_dsa, w_out):
    assert norm_g.shape[0] == 1 and x_sample.shape[1] == 1
    weights = _prep_weights(norm_g[0], w_in[0], q_norm_g[0], k_norm_g[0], idx_k_norm_g[0],
                            w_up_sb[0], w_up_dsa[0], w_out[0])
    p = _prompt_layer(x_prompt, weights, 256, 128)
    caches = (cache_sb_k[0], cache_sb_v[0], cache_dsa_k[0], cache_dsa_v[0], cache_idx_k[0])
    s = _sample_layer(x_sample, caches, page_table, weights)
    return (p[0], s[0], p[1], p[2], p[3], p[4], p[5], s[1], s[2], s[3], s[4], s[5])
```

```python
import functools

import jax
import jax.numpy as jnp
from jax import lax
from jax.experimental import pallas as pl
from jax.experimental.pallas import tpu as pltpu

F32 = jnp.float32
BF16 = jnp.bfloat16
I32 = jnp.int32
I16 = jnp.int16

LANES = 128
HEAD_DIM = 64
HALF = HEAD_DIM // 2
SB_HEADS, SB_KV_HEADS = 8, 4
DSA_HEADS, DSA_KV_HEADS = 8, 2
IDX_HEADS, IDX_DIM = 8, 64
D_MODEL = 1024
PAGE_SIZE = 128
DSA_TOPK_MAX = 256
ROPE_THETA = 10000.0
NORM_EPS = 1e-6
ATTN_SCALE = HEAD_DIM ** -0.5
IDX_SCALE = (IDX_HEADS * IDX_DIM) ** -0.5
INT_MIN = -2 ** 31
NEG_BIG = -1e30
SB_DEAD = -110.0
LOGIT_SAFE = 40.0
VMEM_LIMIT = 56 * 1024 * 1024

_C_QSB, _C_KSB, _C_VSB, _C_ZSB = 0, 512, 768, 1024
_C_QD, _C_KD, _C_VD, _C_ZD = 1536, 2048, 2176, 2304
_C_QI, _C_KI, _C_GSB, _C_GD, _C_END = 2816, 3328, 3456, 4480, 5504
_W_SPLIT = 3400


def _dot(a, b):
    return jnp.dot(a, b, preferred_element_type=F32)


def _dot_nt(a, b):
    return lax.dot_general(a, b, (((1,), (1,)), ((), ())), preferred_element_type=F32)


def _dot_tn(a, b):
    return lax.dot_general(a, b, (((0,), (0,)), ((), ())), preferred_element_type=F32)


def _split_bf16(x):
    hi = x.astype(BF16)
    lo = (x - hi.astype(F32)).astype(BF16)
    return hi, lo


def _feat_kernel(x_ref, g_ref, w_ref, cos_ref, sin_ref, qn_ref, kn_ref, in_ref,
                 ksb_ref, vsb_ref, kd_ref, vd_ref, ki_ref,
                 qsb_ref, zsb_ref, qd_ref, zd_ref, qi_ref, kiw_ref, gsb_ref, gd_ref,
                 kesb_ref, vesb_ref, ked_ref, kei_ref, vdt_ref):
    x = x_ref[...]
    ms = jnp.mean(x * x, axis=-1, keepdims=True)
    xn = (x * lax.rsqrt(ms + NORM_EPS) * g_ref[...]).astype(BF16)

    tm = x.shape[0]
    lane = lax.broadcasted_iota(I32, (tm, LANES), 1)
    lo_half = lane < HEAD_DIM
    lo_rot = (lane % HEAD_DIM) < HALF
    r = lax.broadcasted_iota(I32, (LANES, LANES), 0) // HEAD_DIM
    c = lax.broadcasted_iota(I32, (LANES, LANES), 1) // HEAD_DIM
    group_ones = (r == c).astype(F32)
    cos = cos_ref[...]
    sin = sin_ref[...]

    def proj(c0, c1):
        return _dot(xn, w_ref[:, c0:c1])

    def head_norm(v, gain):
        ss = jnp.dot(v * v, group_ones, precision=lax.Precision.HIGHEST, preferred_element_type=F32)
        return v * lax.rsqrt(ss * (1.0 / HEAD_DIM) + NORM_EPS) * gain

    def rope(v):
        rot = jnp.where(lo_rot, pltpu.roll(v, LANES - HALF, 1), pltpu.roll(v, HALF, 1))
        return v * cos + rot * sin

    def expand(pair, e):
        if e == 0:
            a = jnp.where(lo_half, pair, 0.0)
            return a, pltpu.roll(a, HEAD_DIM, 1)
        b = jnp.where(lo_half, 0.0, pair)
        return pltpu.roll(b, HEAD_DIM, 1), b

    def store_expanded(dst_ref, val, n_heads):
        for h in range(n_heads):
            pair = val[:, (h // 2) * LANES:(h // 2 + 1) * LANES]
            a, b = expand(pair, h % 2)
            dst_ref[:, h * 2 * LANES:h * 2 * LANES + LANES] = a.astype(BF16)
            dst_ref[:, h * 2 * LANES + LANES:(h + 1) * 2 * LANES] = b.astype(BF16)

    qsb_ref[...] = (proj(_C_QSB, _C_KSB) * ATTN_SCALE).astype(BF16)
    ksb = proj(_C_KSB, _C_VSB)
    ksb_ref[...] = ksb
    store_expanded(kesb_ref, ksb, SB_KV_HEADS)
    vsb = proj(_C_VSB, _C_ZSB)
    vsb_ref[...] = vsb
    store_expanded(vesb_ref, vsb, SB_KV_HEADS)
    zsb_ref[...] = proj(_C_ZSB, _C_QD)

    qd = proj(_C_QD, _C_KD)
    for p in range(DSA_HEADS // 2):
        blk = rope(head_norm(qd[:, p * LANES:(p + 1) * LANES], qn_ref[...]))
        qd_ref[:, p * LANES:(p + 1) * LANES] = (blk * ATTN_SCALE).astype(BF16)
    kd = rope(head_norm(proj(_C_KD, _C_VD), kn_ref[...]))
    kd_ref[...] = kd
    store_expanded(ked_ref, kd, DSA_KV_HEADS)
    vd = proj(_C_VD, _C_ZD)
    vd_ref[...] = vd
    vdt_ref[0] = vd.T.astype(BF16)
    zd_ref[...] = proj(_C_ZD, _C_QI)

    qi = proj(_C_QI, _C_KI)
    for p in range(IDX_HEADS // 2):
        qi_ref[:, p * LANES:(p + 1) * LANES] = rope(qi[:, p * LANES:(p + 1) * LANES]).astype(BF16)
    kiw = proj(_C_KI, _C_GSB)
    kiw_ref[...] = kiw
    ki = rope(head_norm(kiw, in_ref[...]))
    ki_ref[...] = ki[:, :IDX_DIM]
    kei_ref[:, :LANES] = ki.astype(BF16)
    kei_ref[:, LANES:] = pltpu.roll(ki, HEAD_DIM, 1).astype(BF16)

    gsb_ref[...] = proj(_C_GSB, _C_GD)
    gd_ref[...] = proj(_C_GD, _C_END)


def _features(x2d, rows_per_pos_cycle, cos, sin, norm_g, w_r, qn, kn, inn, tm):
    R = x2d.shape[0]
    n_pos_blocks = rows_per_pos_cycle // tm
    row = lambda i: (i, 0)
    const = lambda i: (0, 0)
    pos = lambda i: (i % n_pos_blocks, 0)
    f32_widths = dict(ksb=256, vsb=256, kd=128, vd=128, ki=64)
    names = ["ksb", "vsb", "kd", "vd", "ki", "qsb", "zsb", "qd", "zd", "qi", "kiw", "gsb", "gd",
             "kesb", "vesb", "ked", "kei"]
    widths = dict(f32_widths, qsb=512, zsb=512, qd=512, zd=512, qi=512, kiw=128, gsb=1024, gd=1024,
                  kesb=1024, vesb=1024, ked=512, kei=256)
    dtypes = {n: F32 for n in names}
    for n in ("qsb", "qd", "qi", "kesb", "vesb", "ked", "kei"):
        dtypes[n] = BF16
    vdt_rows = DSA_KV_HEADS * HEAD_DIM
    outs = pl.pallas_call(
        _feat_kernel,
        grid=(R // tm,),
        in_specs=[
            pl.BlockSpec((tm, D_MODEL), row),
            pl.BlockSpec((1, D_MODEL), const),
            pl.BlockSpec((D_MODEL, _C_END), const),
            pl.BlockSpec((tm, LANES), pos),
            pl.BlockSpec((tm, LANES), pos),
            pl.BlockSpec((1, LANES), const),
            pl.BlockSpec((1, LANES), const),
            pl.BlockSpec((1, LANES), const),
        ],
        out_specs=[pl.BlockSpec((tm, widths[n]), row) for n in names]
        + [pl.BlockSpec((1, vdt_rows, tm), lambda i: (i, 0, 0))],
        out_shape=[jax.ShapeDtypeStruct((R, widths[n]), dtypes[n]) for n in names]
        + [jax.ShapeDtypeStruct((R // tm, vdt_rows, tm), BF16)],
        compiler_params=pltpu.CompilerParams(dimension_semantics=("arbitrary",),
                                             vmem_limit_bytes=VMEM_LIMIT),
        name="features",
    )(x2d, norm_g, w_r, cos, sin, qn, kn, inn)
    return dict(zip(names + ["vdt"], outs))


def _sb_softplus_parts(z):
    e = jnp.exp(-jnp.abs(z))
    return jnp.maximum(z, 0.0) + jnp.log(1.0 + e)


def _sb_kernel(q_ref, ke_ref, ve_ref, o_ref, carry_ref, acc_ref, *, T):
    i = pl.program_id(1)
    rr = lax.broadcasted_iota(I32, (T, T), 0)
    cc = lax.broadcasted_iota(I32, (T, T), 1)
    suffix = (rr > cc).astype(BF16)
    suffix2 = jnp.concatenate([suffix, suffix], axis=0)
    allowed1 = cc < rr
    allowed = jnp.concatenate([allowed1, allowed1], axis=0)

    def widen(v):
        return v if T == LANES else jnp.concatenate([v] * (T // LANES), axis=1)

    def block(j, diag):
        start = pl.multiple_of(j * T, T)
        heads = range(SB_KV_HEADS)
        zs = []
        for kvh in heads:
            q2 = q_ref[:, kvh * LANES:(kvh + 1) * LANES]
            c0 = kvh * 2 * LANES
            ka = ke_ref[pl.ds(start, T), c0:c0 + LANES]
            kb = ke_ref[pl.ds(start, T), c0 + LANES:c0 + 2 * LANES]
            zs.append(jnp.concatenate([_dot_nt(q2, ka), _dot_nt(q2, kb)], axis=0))
        l1ms = [-_sb_softplus_parts(z) for z in zs]
        if diag:
            l1ms = [jnp.where(allowed, l1m, 0.0) for l1m in l1ms]
        tails = [_dot(jnp.concatenate(_split_bf16(l1m), axis=1), suffix2) for l1m in l1ms]
        tots = [jnp.sum(l1m, axis=1, keepdims=True) for l1m in l1ms]
        live = None
        for kvh in heads:
            c0 = kvh * 2 * LANES
            if diag:
                logit = zs[kvh] + l1ms[kvh] + tails[kvh]
                a = jnp.where(allowed, jnp.exp(logit), 0.0).astype(BF16)
                carry = jnp.broadcast_to(tots[kvh], (2 * T, LANES))
            else:
                logit = zs[kvh] + l1ms[kvh] + tails[kvh] + widen(carry_ref[kvh])
                a = jnp.exp(logit).astype(BF16)
                carry = carry_ref[kvh] + tots[kvh]
            va = ve_ref[pl.ds(start, T), c0:c0 + LANES]
            vb = ve_ref[pl.ds(start, T), c0 + LANES:c0 + 2 * LANES]
            o = _dot(a[:T], va) + _dot(a[T:], vb)
            acc_ref[kvh] = o if diag else acc_ref[kvh] + o
            carry_ref[kvh] = carry
            live = carry if live is None else jnp.maximum(live, carry)
        return jnp.max(live)

    def cond(s):
        return jnp.logical_and(s[0] >= 0, s[1] >= SB_DEAD)

    def body(s):
        return s[0] - 1, block(s[0], False)

    lax.while_loop(cond, body, (i - 1, block(i, True)))
    for kvh in range(SB_KV_HEADS):
        o_ref[:, kvh * LANES:(kvh + 1) * LANES] = acc_ref[kvh]


def _sb_attention(qsb, kesb, vesb, B, S, T):
    nq = S // T
    return pl.pallas_call(
        functools.partial(_sb_kernel, T=T),
        grid=(B, nq),
        in_specs=[
            pl.BlockSpec((T, 512), lambda b, i: (b * nq + i, 0)),
            pl.BlockSpec((S, 1024), lambda b, i: (b, 0)),
            pl.BlockSpec((S, 1024), lambda b, i: (b, 0)),
        ],
        out_specs=pl.BlockSpec((T, 512), lambda b, i: (b * nq + i, 0)),
        out_shape=jax.ShapeDtypeStruct((B * S, 512), F32),
        scratch_shapes=[pltpu.VMEM((SB_KV_HEADS, 2 * T, LANES), F32),
                        pltpu.VMEM((SB_KV_HEADS, T, LANES), F32)],
        compiler_params=pltpu.CompilerParams(dimension_semantics=("arbitrary", "arbitrary"),
                                             vmem_limit_bytes=VMEM_LIMIT),
        name="sb_prompt",
    )(qsb, kesb, vesb)


NEG_INF = float("-inf")
COUNT_WIDE = 2


def _canonical_score(score):
    return jnp.where(score == 0.0, 0.0, score)


def _pattern_to_float(pat):
    return pltpu.bitcast(pat ^ ((pat >> 31) & 0x7FFFFFFF), F32)


def _count_rows(score_ref, n_steps, rows, pred):
    width = score_ref.shape[1]

    def body(j, acc):
        start = pl.multiple_of(j * rows, rows)
        scores = score_ref[pl.ds(start, rows), :]
        idx = start + lax.broadcasted_iota(I32, (rows, width), 0)
        hit = jnp.where(pred(scores, idx), 1, 0).astype(I32)
        return acc + hit.reshape(rows // 8, 8, width).sum(axis=0)

    acc = lax.fori_loop(0, n_steps, body, jnp.zeros((8, width), I32))
    return acc.sum(axis=0, keepdims=True)


def _kth_largest_score(score_ref, n_chunks, chunk, topk):
    width = score_ref.shape[1]
    n_steps = (n_chunks + COUNT_WIDE - 1) // COUNT_WIDE
    rows = COUNT_WIDE * chunk

    def bit_step(k, pat):
        cand = pat ^ jnp.left_shift(jnp.int32(1), 31 - k)
        cand_f = _pattern_to_float(cand)
        cnt = _count_rows(score_ref, n_steps, rows, lambda s, idx: s >= cand_f)
        return jnp.where(cnt >= topk, cand, pat)

    pat = lax.fori_loop(0, 32, bit_step, jnp.full((1, width), INT_MIN, I32))
    few = pat == INT_MIN
    t = jnp.where(few, NEG_INF, _pattern_to_float(pat))
    n_gt = _count_rows(score_ref, n_steps, rows, lambda s, idx: s > t)
    return t, jnp.where(few, 0, topk - n_gt)


def _tie_cutoff(score_ref, n_chunks, chunk, t, need, idx_bits):
    width = score_ref.shape[1]
    n_steps = (n_chunks + COUNT_WIDE - 1) // COUNT_WIDE
    rows = COUNT_WIDE * chunk

    def idx_step(k, c):
        cand = c + jnp.left_shift(jnp.int32(1), idx_bits - 1 - k)
        cnt = _count_rows(score_ref, n_steps, rows,
                          lambda s, idx: jnp.logical_and(s == t, idx < cand))
        return jnp.where(cnt < need, cand, c)

    c = lax.fori_loop(0, idx_bits, idx_step, jnp.zeros((1, width), I32))
    return jnp.where(need > 0, c, -1)


def _dsa_kernel(qi_ref, kiw_ref, kei_ref, qd_ref, ked_ref, vdt_ref, o_ref,
                score_ref, acc_ref, kmax_ref, *, T, CK, topk):
    i = pl.program_id(1)
    n_full = (i * T) // CK
    n_chunks = n_full + 1
    rr = lax.broadcasted_iota(I32, (CK, T), 0)
    cc = lax.broadcasted_iota(I32, (CK, T), 1)
    w_t = kiw_ref[...].T[IDX_DIM:IDX_DIM + IDX_HEADS, :]

    def score_chunk(j, last):
        start = pl.multiple_of(j * CK, CK)
        ka = kei_ref[pl.ds(start, CK), :LANES]
        kb = kei_ref[pl.ds(start, CK), LANES:]
        prods = []
        for p in range(IDX_HEADS // 2):
            qp = qi_ref[:, p * LANES:(p + 1) * LANES]
            prods += [_dot_nt(ka, qp), _dot_nt(kb, qp)]
        sc = jnp.zeros((CK, T), F32)
        for h, s in enumerate(prods):
            sc = sc + w_t[h:h + 1, :] * jnp.maximum(s, 0.0)
        sc = _canonical_score(sc * IDX_SCALE)
        if last:
            sc = jnp.where(start + rr <= i * T + cc, sc, NEG_INF)
        score_ref[pl.ds(start, CK), :] = sc

    def score_body(j, carry):
        score_chunk(j, False)
        return carry

    lax.fori_loop(0, n_full, score_body, 0)
    score_chunk(n_full, True)

    @pl.when(n_chunks % COUNT_WIDE != 0)
    def _():
        pad_start = pl.multiple_of(n_chunks * CK, CK)
        score_ref[pl.ds(pad_start, CK), :] = jnp.full((CK, T), NEG_INF, F32)

    t, need = _kth_largest_score(score_ref, n_chunks, CK, topk)
    need_f = need.astype(F32)

    acc_ref[...] = jnp.zeros(acc_ref.shape, F32)
    top_rows = lax.broadcasted_iota(I32, (LANES, T), 0) < HEAD_DIM
    r2 = lax.broadcasted_iota(I32, (CK, CK), 0)
    c2 = lax.broadcasted_iota(I32, (CK, CK), 1)
    prefix = (c2 <= r2).astype(BF16)

    def chunk_inputs(j, tied_before):
        start = pl.multiple_of(j * CK, CK)
        sc = score_ref[pl.ds(start, CK), :]
        tied = sc == t
        rank = tied_before + _dot(prefix, jnp.where(tied, 1.0, 0.0).astype(BF16))
        sel = jnp.logical_or(sc > t, jnp.logical_and(tied, rank <= need_f))
        return start, sel, rank[CK - 1:CK, :]

    def pair_inputs(j, start, p):
        kvh = p // 2
        kcol = kvh * 2 * LANES
        qp = qd_ref[:, p * LANES:(p + 1) * LANES]
        vt = vdt_ref[j, kvh * HEAD_DIM:(kvh + 1) * HEAD_DIM, :]
        kblks = [ked_ref[pl.ds(start, CK), kcol + e * LANES:kcol + (e + 1) * LANES] for e in range(2)]
        return qp, vt, kblks

    def attend_online(j, state):
        ms, ls, tied_before = state
        start, sel, tied_before = chunk_inputs(j, tied_before)
        new_ms, new_ls = [], []
        for p in range(DSA_HEADS // 2):
            qp, vt, kblks = pair_inputs(j, start, p)
            outs, alphas = [], []
            for e in range(2):
                h = 2 * p + e
                lg = jnp.where(sel, _dot_nt(kblks[e], qp), NEG_BIG)
                m_new = jnp.maximum(ms[h], jnp.max(lg, axis=0, keepdims=True))
                alpha = jnp.exp(ms[h] - m_new)
                pr = jnp.where(sel, jnp.exp(lg - m_new), 0.0)
                new_ls.append(alpha * ls[h] + jnp.sum(pr, axis=0, keepdims=True))
                new_ms.append(m_new)
                outs.append(_dot(vt, pr.astype(BF16)))
                alphas.append(alpha)
            scale = jnp.where(top_rows, alphas[0], alphas[1])
            acc_ref[p] = acc_ref[p] * scale + jnp.concatenate(outs, axis=0)
        return tuple(new_ms), tuple(new_ls), tied_before

    def softmax_online(_):
        init = (tuple(jnp.full((1, T), NEG_BIG, F32) for _ in range(DSA_HEADS)),
                tuple(jnp.zeros((1, T), F32) for _ in range(DSA_HEADS)),
                jnp.zeros((1, T), F32))
        return lax.fori_loop(0, n_chunks, attend_online, init)[1]

    def attend_unshifted(j, state):
        ls, tied_before = state
        start, sel, tied_before = chunk_inputs(j, tied_before)
        pairs = [pair_inputs(j, start, p) for p in range(DSA_HEADS // 2)]
        lgs = [_dot_nt(kblks[e], qp) for qp, _, kblks in pairs for e in range(2)]
        prs = [jnp.where(sel, jnp.exp(lg), 0.0) for lg in lgs]
        new_ls = tuple(l + pr.reshape(CK // 8, 8, T).sum(axis=0) for l, pr in zip(ls, prs))
        for p, (_, vt, _) in enumerate(pairs):
            outs = [_dot(vt, prs[2 * p + e].astype(BF16)) for e in range(2)]
            acc_ref[p] = acc_ref[p] + jnp.concatenate(outs, axis=0)
        return new_ls, tied_before

    def softmax_unshifted(_):
        init = (tuple(jnp.zeros((8, T), F32) for _ in range(DSA_HEADS)), jnp.zeros((1, T), F32))
        ls = lax.fori_loop(0, n_chunks, attend_unshifted, init)[0]
        return tuple(l.sum(axis=0, keepdims=True) for l in ls)

    @pl.when(i == 0)
    def _():
        for kvh in range(DSA_KV_HEADS):
            def k_norm(c, m):
                rows = ked_ref[pl.ds(pl.multiple_of(c * CK, CK), CK),
                               kvh * 2 * LANES:kvh * 2 * LANES + LANES].astype(F32)
                return jnp.maximum(m, jnp.max(jnp.sum(rows * rows, axis=1, keepdims=True)))
            kmax_ref[kvh] = lax.fori_loop(0, ked_ref.shape[0] // CK, k_norm, jnp.float32(0.0))

    bound2 = jnp.float32(0.0)
    for p in range(DSA_HEADS // 2):
        qf = qd_ref[:, p * LANES:(p + 1) * LANES].astype(F32)
        q2max = jnp.max(jnp.sum(qf * qf, axis=1, keepdims=True))
        bound2 = jnp.maximum(bound2, q2max * kmax_ref[p // 2])
    ls = lax.cond(bound2 <= LOGIT_SAFE * LOGIT_SAFE, softmax_unshifted, softmax_online, 0)
    for p in range(DSA_HEADS // 2):
        denom = jnp.where(top_rows, ls[2 * p], ls[2 * p + 1])
        o_ref[:, p * LANES:(p + 1) * LANES] = (acc_ref[p] / denom).T


def _dsa_attention(qi, kiw, kei, qd, ked, vdt, B, S, T, topk):
    nq = S // T
    CK = vdt.shape[2]
    assert S % (COUNT_WIDE * CK) == 0 and CK % T == 0
    qrow = lambda b, i: (b * nq + i, 0)
    batch = lambda b, i: (b, 0)
    return pl.pallas_call(
        functools.partial(_dsa_kernel, T=T, CK=CK, topk=topk),
        grid=(B, nq),
        in_specs=[
            pl.BlockSpec((T, 512), qrow),
            pl.BlockSpec((T, LANES), qrow),
            pl.BlockSpec((S, 256), batch),
            pl.BlockSpec((T, 512), qrow),
            pl.BlockSpec((S, 512), batch),
            pl.BlockSpec((S // CK, LANES, CK), lambda b, i: (b, 0, 0)),
        ],
        out_specs=pl.BlockSpec((T, 512), qrow),
        out_shape=jax.ShapeDtypeStruct((B * S, 512), F32),
        scratch_shapes=[
            pltpu.VMEM((S, T), F32),
            pltpu.VMEM((DSA_HEADS // 2, LANES, T), F32),
            pltpu.SMEM((DSA_KV_HEADS,), F32),
        ],
        compiler_params=pltpu.CompilerParams(dimension_semantics=("arbitrary", "arbitrary"),
                                             vmem_limit_bytes=VMEM_LIMIT),
        name="dsa_prompt",
    )(qi, kiw, kei, qd, ked, vdt)


def _merge_kernel(x_ref, osb_ref, zsb_ref, od_ref, zd_ref, gsb_ref, gd_ref,
                  wsb_ref, wd_ref, wo_ref, y_ref):
    zsb = zsb_ref[...]
    zd = zd_ref[...]
    a_sb = (osb_ref[...] * (zsb * jax.nn.sigmoid(zsb))).astype(BF16)
    a_d = (od_ref[...] * (zd * jax.nn.sigmoid(zd))).astype(BF16)
    u_sb = _dot(a_sb, wsb_ref[...])
    u_d = _dot(a_d, wd_ref[...])
    mixed = jax.nn.sigmoid(gsb_ref[...]) * u_sb + jax.nn.sigmoid(gd_ref[...]) * u_d
    y_ref[...] = x_ref[...] + _dot(mixed.astype(BF16), wo_ref[...])


def _merge(x2d, osb, zsb, od, zd, gsb, gd, wsb, wd, wo, tm):
    R = x2d.shape[0]
    row = lambda i: (i, 0)
    const = lambda i: (0, 0)
    return pl.pallas_call(
        _merge_kernel,
        grid=(R // tm,),
        in_specs=[
            pl.BlockSpec((tm, D_MODEL), row),
            pl.BlockSpec((tm, 512), row),
            pl.BlockSpec((tm, 512), row),
            pl.BlockSpec((tm, 512), row),
            pl.BlockSpec((tm, 512), row),
            pl.BlockSpec((tm, D_MODEL), row),
            pl.BlockSpec((tm, D_MODEL), row),
            pl.BlockSpec((512, D_MODEL), const),
            pl.BlockSpec((512, D_MODEL), const),
            pl.BlockSpec((D_MODEL, D_MODEL), const),
        ],
        out_specs=pl.BlockSpec((tm, D_MODEL), row),
        out_shape=jax.ShapeDtypeStruct((R, D_MODEL), F32),
        compiler_params=pltpu.CompilerParams(dimension_semantics=("arbitrary",),
                                             vmem_limit_bytes=VMEM_LIMIT),
        name="merge",
    )(x2d, osb, zsb, od, zd, gsb, gd, wsb, wd, wo)


def _rope_tables(pos):
    inv = ROPE_THETA ** (-jnp.arange(HALF, dtype=F32) / HALF)
    ang = pos.astype(F32)[:, None] * inv[None, :]
    cos = jnp.cos(ang)
    sin = jnp.sin(ang)
    return jnp.tile(cos, (1, 4)), jnp.tile(jnp.concatenate([-sin, sin], axis=1), (1, 2))


def _prep_weights(norm_g, w_in, q_norm_g, k_norm_g, idx_k_norm_g, w_up_sb, w_up_dsa, w_out):
    pad = jnp.zeros((D_MODEL, _C_GSB - _C_KI - (_W_SPLIT - _C_KI)), w_in.dtype)
    w_r = jnp.concatenate([w_in[:, :_W_SPLIT], pad, w_in[:, _W_SPLIT:]], axis=1).astype(BF16)
    qn = jnp.tile(q_norm_g, 2)[None, :]
    kn = jnp.tile(k_norm_g, 2)[None, :]
    inn = jnp.concatenate([idx_k_norm_g, jnp.zeros((LANES - IDX_DIM,), F32)])[None, :]
    return (norm_g[None, :], w_r, qn, kn, inn,
            w_up_sb.astype(BF16), w_up_dsa.astype(BF16), w_out.astype(BF16))


def _prompt_layer(x, weights, t_feat, t_attn):
    B, S, _ = x.shape
    norm_g, w_r, qn, kn, inn, wsb, wd, wo = weights
    x2d = x.reshape(B * S, D_MODEL)
    cos, sin = _rope_tables(jnp.arange(S))
    f = _features(x2d, S, cos, sin, norm_g, w_r, qn, kn, inn, t_feat)
    topk = min(DSA_TOPK_MAX, S // 4)
    osb = _sb_attention(f["qsb"], f["kesb"], f["vesb"], B, S, t_attn)
    od = _dsa_attention(f["qi"], f["kiw"], f["kei"], f["qd"], f["ked"], f["vdt"], B, S, LANES, topk)
    y = _merge(x2d, osb, f["zsb"], od, f["zd"], f["gsb"], f["gd"], wsb, wd, wo, t_feat)
    return (y.reshape(B, S, D_MODEL),
            f["ksb"].reshape(1, B, S, SB_KV_HEADS, HEAD_DIM),
            f["vsb"].reshape(1, B, S, SB_KV_HEADS, HEAD_DIM),
            f["kd"].reshape(1, B, S, DSA_KV_HEADS, HEAD_DIM),
            f["vd"].reshape(1, B, S, DSA_KV_HEADS, HEAD_DIM),
            f["ki"].reshape(1, B, S, IDX_DIM))


PAGE_UNROLL = 32


def _page_copy(cache_hbm, pt_ref, buf, sems, ci, seq, slot, p):
    return pltpu.make_async_copy(cache_hbm.at[pt_ref[seq, p]], buf.at[slot, p], sems.at[ci, slot])


def _stream_pages(caches, pt_ref, bufs, sems, n_pages):
    b = pl.program_id(0)
    nb = pl.num_programs(0)
    slot = b % 2

    def start_all_pages(seq, slot_):
        def body(p, carry):
            for ci, (cache, buf) in enumerate(zip(caches, bufs)):
                _page_copy(cache, pt_ref, buf, sems, ci, seq, slot_, p).start()
            return carry
        lax.fori_loop(0, n_pages, body, 0, unroll=min(8, n_pages))

    @pl.when(b == 0)
    def _():
        start_all_pages(0, 0)

    @pl.when(b + 1 < nb)
    def _():
        start_all_pages(b + 1, 1 - slot)

    for p in range(n_pages):
        for ci, (cache, buf) in enumerate(zip(caches, bufs)):
            _page_copy(cache, pt_ref, buf, sems, ci, b, slot, p).wait()
    return slot


def _dec_sb_kernel(pt_ref, qm_ref, k_hbm, v_hbm, o_ref, kfirst, vfirst, kmore, vmore, sems, *, n_pages, G):
    b = pl.program_id(0)
    nb = pl.num_programs(0)
    slot = b % 2
    n_groups = n_pages // G
    qm = qm_ref[0]
    P = PAGE_SIZE
    rr = lax.broadcasted_iota(I32, (P, P), 0)
    cc = lax.broadcasted_iota(I32, (P, P), 1)
    suffix = (rr > cc).astype(BF16)
    ones = jnp.ones((P, LANES), BF16)

    def group_copies(seq, g, kdst, vdst, ksem, vsem):
        copies = []
        for r in range(G):
            page = pt_ref[seq, n_pages - (g + 1) * G + r]
            copies.append(pltpu.make_async_copy(k_hbm.at[page], kdst.at[r], ksem))
            copies.append(pltpu.make_async_copy(v_hbm.at[page], vdst.at[r], vsem))
        return copies

    def first_copies(seq, s):
        return group_copies(seq, 0, kfirst.at[s], vfirst.at[s], sems.at[0, s], sems.at[1, s])

    def process(kb, vb, carry, acc):
        zs, l1ms, tails, tots = [], [], [], []
        for r in range(G):
            z = _dot(qm, kb[r].astype(BF16))
            l1m = -_sb_softplus_parts(z)
            hi, lo = _split_bf16(l1m)
            zs.append(z)
            l1ms.append(l1m)
            tails.append(_dot(hi, suffix) + _dot(lo, suffix))
            tots.append(_dot(hi, ones) + _dot(lo, ones))
        for r in reversed(range(G)):
            a = jnp.exp(zs[r] + l1ms[r] + tails[r] + carry).astype(BF16)
            acc = acc + _dot_nt(a, vb[r].astype(BF16))
            carry = carry + tots[r]
        return carry, acc

    @pl.when(b == 0)
    def _():
        for c in first_copies(0, 0):
            c.start()

    @pl.when(b + 1 < nb)
    def _():
        for c in first_copies(b + 1, 1 - slot):
            c.start()

    for c in first_copies(b, slot):
        c.wait()
    n_heads = qm.shape[0]
    carry, acc = process(kfirst.at[slot], vfirst.at[slot],
                         jnp.zeros((n_heads, LANES), F32), jnp.zeros((n_heads, qm.shape[1]), F32))

    def cond(s):
        return jnp.logical_and(s[0] < n_groups, s[3] >= SB_DEAD)

    def body(s):
        g, carry, acc, _ = s
        copies = group_copies(b, g, kmore, vmore, sems.at[2, 0], sems.at[2, 1])
        for c in copies:
            c.start()
        for c in copies:
            c.wait()
        carry, acc = process(kmore, vmore, carry, acc)
        return g + 1, carry, acc, jnp.max(carry)

    _, _, acc, _ = lax.while_loop(cond, body, (jnp.int32(1), carry, acc, jnp.max(carry)))
    o_ref[0] = acc


def _dec_sb(page_table, qm, cache_k, cache_v, group):
    DB, n_pages = page_table.shape
    width = cache_k.shape[1]
    return pl.pallas_call(
        functools.partial(_dec_sb_kernel, n_pages=n_pages, G=group),
        grid_spec=pltpu.PrefetchScalarGridSpec(
            num_scalar_prefetch=1,
            grid=(DB,),
            in_specs=[
                pl.BlockSpec((1, SB_HEADS, width), lambda b, pt: (b, 0, 0)),
                pl.BlockSpec(memory_space=pl.ANY),
                pl.BlockSpec(memory_space=pl.ANY),
            ],
            out_specs=pl.BlockSpec((1, SB_HEADS, width), lambda b, pt: (b, 0, 0)),
            scratch_shapes=[
                pltpu.VMEM((2, group, width, PAGE_SIZE), F32),
                pltpu.VMEM((2, group, width, PAGE_SIZE), F32),
                pltpu.VMEM((group, width, PAGE_SIZE), F32),
                pltpu.VMEM((group, width, PAGE_SIZE), F32),
                pltpu.SemaphoreType.DMA((3, 2)),
            ]),
        out_shape=jax.ShapeDtypeStruct((DB, SB_HEADS, width), F32),
        compiler_params=pltpu.CompilerParams(dimension_semantics=("arbitrary",),
                                             vmem_limit_bytes=VMEM_LIMIT),
        name="dec_sb",
    )(page_table, qm, cache_k, cache_v)


def _dec_idx_kernel(pt_ref, qi_ref, w_ref, kinew_ref, k_hbm, score_ref, kbuf, sems, *, n_pages, n_rows):
    slot = _stream_pages((k_hbm,), pt_ref, (kbuf,), sems, n_pages)
    qi = qi_ref[0]
    w = w_ref[0]

    def score(pp, carry):
        kk = jnp.concatenate([kbuf[slot, 2 * pp], kbuf[slot, 2 * pp + 1]], axis=1).astype(BF16)
        s = _dot(qi, kk)
        sc = _canonical_score(jnp.sum(w * jnp.maximum(s, 0.0), axis=0, keepdims=True) * IDX_SCALE)
        score_ref[0, pl.ds(2 * pp, 1), :] = sc[:, :PAGE_SIZE]
        score_ref[0, pl.ds(2 * pp + 1, 1), :] = sc[:, PAGE_SIZE:]
        return carry

    lax.fori_loop(0, n_pages // 2, score, 0, unroll=min(PAGE_UNROLL // 2, n_pages // 2))
    s_new = jnp.sum(qi.astype(F32) * kinew_ref[0].astype(F32), axis=1, keepdims=True)
    sc_new = jnp.sum(w * jnp.maximum(s_new, 0.0), axis=0, keepdims=True) * IDX_SCALE
    tail_rows = n_rows - n_pages
    r = lax.broadcasted_iota(I32, (tail_rows, LANES), 0)
    c = lax.broadcasted_iota(I32, (tail_rows, LANES), 1)
    new_row = jnp.broadcast_to(_canonical_score(sc_new), (tail_rows, LANES))
    score_ref[0, n_pages:, :] = jnp.where(jnp.logical_and(r == 0, c == 0), new_row, NEG_INF)


def _dec_idx(page_table, qi, w, kinew, cache_k, n_rows):
    DB, n_pages = page_table.shape
    per_seq = lambda b, pt: (b, 0, 0)
    return pl.pallas_call(
        functools.partial(_dec_idx_kernel, n_pages=n_pages, n_rows=n_rows),
        grid_spec=pltpu.PrefetchScalarGridSpec(
            num_scalar_prefetch=1,
            grid=(DB,),
            in_specs=[
                pl.BlockSpec((1, IDX_HEADS, IDX_DIM), per_seq),
                pl.BlockSpec((1, IDX_HEADS, 1), per_seq),
                pl.BlockSpec((1, 1, IDX_DIM), per_seq),
                pl.BlockSpec(memory_space=pl.ANY),
            ],
            out_specs=pl.BlockSpec((1, n_rows, LANES), per_seq),
            scratch_shapes=[
                pltpu.VMEM((2, n_pages, IDX_DIM, PAGE_SIZE), F32),
                pltpu.SemaphoreType.DMA((1, 2)),
            ]),
        out_shape=jax.ShapeDtypeStruct((DB, n_rows, LANES), F32),
        compiler_params=pltpu.CompilerParams(dimension_semantics=("arbitrary",),
                                             vmem_limit_bytes=VMEM_LIMIT),
        name="dec_idx",
    )(page_table, qi, w, kinew, cache_k)


def _dec_thr_kernel(scores_ref, t_ref, c_ref, score_scr, *, n_rows, topk, idx_bits):
    for p in range(n_rows):
        score_scr[p * LANES:(p + 1) * LANES, :] = scores_ref[:, p * LANES:(p + 1) * LANES].T
    n_chunks, chunk = n_rows // 2, 2 * LANES
    t, need = _kth_largest_score(score_scr, n_chunks, chunk, topk)
    t_ref[...] = t
    c_ref[...] = _tie_cutoff(score_scr, n_chunks, chunk, t, need, idx_bits)


def _dec_thr(scores2d, n_rows, topk):
    DB = scores2d.shape[0]
    assert n_rows % (2 * COUNT_WIDE) == 0
    idx_bits = (n_rows * LANES - 1).bit_length()
    return pl.pallas_call(
        functools.partial(_dec_thr_kernel, n_rows=n_rows, topk=topk, idx_bits=idx_bits),
        out_shape=[jax.ShapeDtypeStruct((1, DB), F32), jax.ShapeDtypeStruct((1, DB), I32)],
        scratch_shapes=[pltpu.VMEM((n_rows * LANES, DB), F32)],
        compiler_params=pltpu.CompilerParams(vmem_limit_bytes=VMEM_LIMIT),
        name="dec_thr",
    )(scores2d)


def _dec_dsa_kernel(pt_ref, c_ref, t_ref, qm_ref, scores_ref, kdnew_ref, vdnew_ref, k_hbm, v_hbm, o_ref,
                    kbuf, vbuf, lbuf, sems, *, n_pages):
    b = pl.program_id(0)
    slot = _stream_pages((k_hbm, v_hbm), pt_ref, (kbuf, vbuf), sems, n_pages)
    qm = qm_ref[0]
    t = t_ref[0]
    cidx = c_ref[b]
    lane = lax.broadcasted_iota(I32, (1, LANES), 1)

    def selected(p):
        sc = scores_ref[0, pl.ds(p, 1), :]
        idx = p * PAGE_SIZE + lane
        return jnp.logical_or(sc > t, jnp.logical_and(sc == t, idx <= cidx))

    def page_pair(buf, pp):
        return jnp.concatenate([buf[slot, 2 * pp], buf[slot, 2 * pp + 1]], axis=1).astype(BF16)

    def logits(pp, m):
        sel = jnp.concatenate([selected(2 * pp), selected(2 * pp + 1)], axis=1)
        lg = jnp.where(sel, _dot(qm, page_pair(kbuf, pp)), NEG_BIG)
        lbuf[pp] = lg
        return jnp.maximum(m, lg)

    n_heads = qm.shape[0]
    n_pairs = n_pages // 2
    unroll = min(PAGE_UNROLL // 2, n_pairs)
    m = lax.fori_loop(0, n_pairs, logits, jnp.full((n_heads, 2 * LANES), NEG_BIG, F32), unroll=unroll)
    sel_new = selected(n_pages)[:, :1]
    lg_new = jnp.sum(qm.astype(F32) * kdnew_ref[0].astype(F32), axis=1, keepdims=True)
    lg_new = jnp.where(sel_new, lg_new, NEG_BIG)
    m = jnp.maximum(jnp.max(m, axis=1, keepdims=True), lg_new)

    def attend(pp, state):
        l, acc = state
        pr = jnp.exp(lbuf[pp] - m)
        return l + pr, acc + _dot_nt(pr.astype(BF16), page_pair(vbuf, pp))

    l, acc = lax.fori_loop(0, n_pairs, attend,
                           (jnp.zeros((n_heads, 2 * LANES), F32), jnp.zeros((n_heads, LANES), F32)),
                           unroll=unroll)
    pr_new = jnp.where(sel_new, jnp.exp(lg_new - m), 0.0)
    pr_new_b = pr_new.astype(BF16).astype(F32)
    acc = acc + pr_new_b * vdnew_ref[0].astype(F32)
    denom = jnp.sum(l, axis=1, keepdims=True) + pr_new
    o_ref[0] = acc / denom


def _dec_dsa(page_table, cidx, t_rep, qm, scores, kdnew, vdnew, cache_k, cache_v):
    DB, n_pages = page_table.shape
    n_rows = scores.shape[1]
    per_seq = lambda b, pt, c: (b, 0, 0)
    return pl.pallas_call(
        functools.partial(_dec_dsa_kernel, n_pages=n_pages),
        grid_spec=pltpu.PrefetchScalarGridSpec(
            num_scalar_prefetch=2,
            grid=(DB,),
            in_specs=[
                pl.BlockSpec((1, 1, LANES), per_seq),
                pl.BlockSpec((1, DSA_HEADS, LANES), per_seq),
                pl.BlockSpec((1, n_rows, LANES), per_seq),
                pl.BlockSpec((1, 1, LANES), per_seq),
                pl.BlockSpec((1, 1, LANES), per_seq),
                pl.BlockSpec(memory_space=pl.ANY),
                pl.BlockSpec(memory_space=pl.ANY),
            ],
            out_specs=pl.BlockSpec((1, DSA_HEADS, LANES), per_seq),
            scratch_shapes=[
                pltpu.VMEM((2, n_pages, PAGE_SIZE, LANES), F32),
                pltpu.VMEM((2, n_pages, PAGE_SIZE, LANES), F32),
                pltpu.VMEM((n_pages // 2, DSA_HEADS, 2 * LANES), F32),
                pltpu.SemaphoreType.DMA((2, 2)),
            ]),
        out_shape=jax.ShapeDtypeStruct((DB, DSA_HEADS, LANES), F32),
        compiler_params=pltpu.CompilerParams(dimension_semantics=("arbitrary",),
                                             vmem_limit_bytes=VMEM_LIMIT),
        name="dec_dsa",
    )(page_table, cidx, t_rep, qm, scores, kdnew, vdnew, cache_k, cache_v)


def _heads_in_kv_lanes(q, n_heads, n_kv):
    R = q.shape[0]
    q3 = q.reshape(R, n_heads, 1, HEAD_DIM)
    kv_of_head = jnp.arange(n_heads) // (n_heads // n_kv)
    onehot = (kv_of_head[:, None] == jnp.arange(n_kv)[None, :])[None, :, :, None]
    return jnp.where(onehot, q3, jnp.zeros((), q.dtype)).reshape(R, n_heads, n_kv * HEAD_DIM)


def _own_kv_lanes(o, n_heads, n_kv):
    R = o.shape[0]
    o4 = o.reshape(R, n_heads, n_kv, HEAD_DIM)
    group = n_heads // n_kv
    return jnp.concatenate([o4[:, h, h // group, :] for h in range(n_heads)], axis=1)


def _sample_layer(x, caches, page_table, weights):
    DB, T, _ = x.shape
    c_sb_k, c_sb_v, c_dsa_k, c_dsa_v, c_idx_k = caches
    norm_g, w_r, qn, kn, inn, wsb, wd, wo = weights
    n_pages = page_table.shape[1]
    past = n_pages * PAGE_SIZE
    x2d = x.reshape(DB * T, D_MODEL)
    cos, sin = _rope_tables(jnp.full((DB * T,), past))
    f = _features(x2d, DB * T, cos, sin, norm_g, w_r, qn, kn, inn, DB * T)

    n_pool = c_sb_k.shape[0]

    def pages_t(c):
        return jnp.moveaxis(c, 1, -1).reshape(n_pool, -1, PAGE_SIZE)

    qm_sb = _heads_in_kv_lanes(f["qsb"], SB_HEADS, SB_KV_HEADS)
    o_sb = _dec_sb(page_table, qm_sb, pages_t(c_sb_k), pages_t(c_sb_v), 2)
    osb = _own_kv_lanes(o_sb, SB_HEADS, SB_KV_HEADS)

    n_rows = n_pages + 8
    qi3 = f["qi"].reshape(DB, IDX_HEADS, IDX_DIM)
    w3 = f["kiw"][:, IDX_DIM:IDX_DIM + IDX_HEADS].reshape(DB, IDX_HEADS, 1)
    kinew = f["kei"][:, :IDX_DIM].reshape(DB, 1, IDX_DIM)
    scores = _dec_idx(page_table, qi3, w3, kinew, pages_t(c_idx_k), n_rows)
    topk = min(DSA_TOPK_MAX, (past + T) // 4)
    t, cidx = _dec_thr(scores.reshape(DB, n_rows * LANES), n_rows, topk)
    t_rep = jnp.broadcast_to(t.reshape(DB, 1, 1), (DB, 1, LANES))

    qm_d = _heads_in_kv_lanes(f["qd"], DSA_HEADS, DSA_KV_HEADS)
    kdnew = f["kd"].astype(BF16).reshape(DB, 1, LANES)
    vdnew = f["vd"].astype(BF16).reshape(DB, 1, LANES)
    o_d = _dec_dsa(page_table, cidx.reshape(DB), t_rep, qm_d, scores, kdnew, vdnew,
                   pages_t(c_dsa_k), pages_t(c_dsa_v))
    od = _own_kv_lanes(o_d, DSA_HEADS, DSA_KV_HEADS)

    y = _merge(x2d, osb, f["zsb"], od, f["zd"], f["gsb"], f["gd"], wsb, wd, wo, DB * T)
    return (y.reshape(DB, T, D_MODEL),
            f["ksb"].reshape(1, DB, T, SB_KV_HEADS, HEAD_DIM),
            f["vsb"].reshape(1, DB, T, SB_KV_HEADS, HEAD_DIM),
            f["kd"].reshape(1, DB, T, DSA_KV_HEADS, HEAD_DIM),
            f["vd"].reshape(1, DB, T, DSA_KV_HEADS, HEAD_DIM),
            f["ki"].reshape(1, DB, T, IDX_DIM))


def kernel(x_prompt, x_sample, cache_sb_k, cache_sb_v, cache_dsa_k, cache_dsa_v, cache_idx_k, page_table,
           norm_g, w_in, q_norm_g, k_norm_g, idx_k_norm_g, w_up_sb, w_up_dsa, w_out):
    assert norm_g.shape[0] == 1 and x_sample.shape[1] == 1
    weights = _prep_weights(norm_g[0], w_in[0], q_norm_g[0], k_norm_g[0], idx_k_norm_g[0],
                            w_up_sb[0], w_up_dsa[0], w_out[0])
    p = _prompt_layer(x_prompt, weights, 256, 128)
    caches = (cache_sb_k[0], cache_sb_v[0], cache_dsa_k[0], cache_dsa_v[0], cache_idx_k[0])
    s = _sample_layer(x_sample, caches, page_table, weights)
    return (p[0], s[0], p[1], p[2], p[3], p[4], p[5], s[1], s[2], s[3], s[4], s[5])
```

```python
import functools

import jax
import jax.numpy as jnp
from jax import lax
from jax.experimental import pallas as pl
from jax.experimental.pallas import tpu as pltpu

F32 = jnp.float32
BF16 = jnp.bfloat16
I32 = jnp.int32
I16 = jnp.int16

LANES = 128
HEAD_DIM = 64
HALF = HEAD_DIM // 2
SB_HEADS, SB_KV_HEADS = 8, 4
DSA_HEADS, DSA_KV_HEADS = 8, 2
IDX_HEADS, IDX_DIM = 8, 64
D_MODEL = 1024
PAGE_SIZE = 128
DSA_TOPK_MAX = 256
ROPE_THETA = 10000.0
NORM_EPS = 1e-6
ATTN_SCALE = HEAD_DIM ** -0.5
IDX_SCALE = (IDX_HEADS * IDX_DIM) ** -0.5
INT_MIN = -2 ** 31
NEG_BIG = -1e30
SB_DEAD = -110.0
LOGIT_SAFE = 40.0
VMEM_LIMIT = 56 * 1024 * 1024

_C_QSB, _C_KSB, _C_VSB, _C_ZSB = 0, 512, 768, 1024
_C_QD, _C_KD, _C_VD, _C_ZD = 1536, 2048, 2176, 2304
_C_QI, _C_KI, _C_GSB, _C_GD, _C_END = 2816, 3328, 3456, 4480, 5504
_W_SPLIT = 3400


def _dot(a, b):
    return jnp.dot(a, b, preferred_element_type=F32)


def _dot_nt(a, b):
    return lax.dot_general(a, b, (((1,), (1,)), ((), ())), preferred_element_type=F32)


def _dot_tn(a, b):
    return lax.dot_general(a, b, (((0,), (0,)), ((), ())), preferred_element_type=F32)


def _split_bf16(x):
    hi = x.astype(BF16)
    lo = (x - hi.astype(F32)).astype(BF16)
    return hi, lo


def _feat_kernel(x_ref, g_ref, w_ref, cos_ref, sin_ref, qn_ref, kn_ref, in_ref,
                 ksb_ref, vsb_ref, kd_ref, vd_ref, ki_ref,
                 qsb_ref, zsb_ref, qd_ref, zd_ref, qi_ref, kiw_ref, gsb_ref, gd_ref,
                 kesb_ref, vesb_ref, ked_ref, kei_ref, vdt_ref, *, feature_major):
    def put_rows(ref, val):
        if feature_major:
            ref[0] = val.T
        else:
            ref[...] = val

    x = x_ref[...]
    ms = jnp.mean(x * x, axis=-1, keepdims=True)
    xn = (x * lax.rsqrt(ms + NORM_EPS) * g_ref[...]).astype(BF16)

    tm = x.shape[0]
    lane = lax.broadcasted_iota(I32, (tm, LANES), 1)
    lo_half = lane < HEAD_DIM
    lo_rot = (lane % HEAD_DIM) < HALF
    r = lax.broadcasted_iota(I32, (LANES, LANES), 0) // HEAD_DIM
    c = lax.broadcasted_iota(I32, (LANES, LANES), 1) // HEAD_DIM
    group_ones = (r == c).astype(F32)
    cos = cos_ref[...]
    sin = sin_ref[...]

    def proj(c0, c1):
        return _dot(xn, w_ref[:, c0:c1])

    def head_norm(v, gain):
        ss = jnp.dot(v * v, group_ones, precision=lax.Precision.HIGHEST, preferred_element_type=F32)
        return v * lax.rsqrt(ss * (1.0 / HEAD_DIM) + NORM_EPS) * gain

    def rope(v):
        rot = jnp.where(lo_rot, pltpu.roll(v, LANES - HALF, 1), pltpu.roll(v, HALF, 1))
        return v * cos + rot * sin

    def expand(pair, e):
        if e == 0:
            a = jnp.where(lo_half, pair, 0.0)
            return a, pltpu.roll(a, HEAD_DIM, 1)
        b = jnp.where(lo_half, 0.0, pair)
        return pltpu.roll(b, HEAD_DIM, 1), b

    def store_expanded(dst_ref, val, n_heads):
        for h in range(n_heads):
            pair = val[:, (h // 2) * LANES:(h // 2 + 1) * LANES]
            a, b = expand(pair, h % 2)
            dst_ref[:, h * 2 * LANES:h * 2 * LANES + LANES] = a.astype(BF16)
            dst_ref[:, h * 2 * LANES + LANES:(h + 1) * 2 * LANES] = b.astype(BF16)

    qsb_ref[...] = (proj(_C_QSB, _C_KSB) * ATTN_SCALE).astype(BF16)
    ksb = proj(_C_KSB, _C_VSB)
    put_rows(ksb_ref, ksb)
    store_expanded(kesb_ref, ksb, SB_KV_HEADS)
    vsb = proj(_C_VSB, _C_ZSB)
    put_rows(vsb_ref, vsb)
    store_expanded(vesb_ref, vsb, SB_KV_HEADS)
    zsb_ref[...] = proj(_C_ZSB, _C_QD)

    qd = proj(_C_QD, _C_KD)
    for p in range(DSA_HEADS // 2):
        blk = rope(head_norm(qd[:, p * LANES:(p + 1) * LANES], qn_ref[...]))
        qd_ref[:, p * LANES:(p + 1) * LANES] = (blk * ATTN_SCALE).astype(BF16)
    kd = rope(head_norm(proj(_C_KD, _C_VD), kn_ref[...]))
    put_rows(kd_ref, kd)
    store_expanded(ked_ref, kd, DSA_KV_HEADS)
    vd = proj(_C_VD, _C_ZD)
    put_rows(vd_ref, vd)
    vdt_ref[0] = vd.T.astype(BF16)
    zd_ref[...] = proj(_C_ZD, _C_QI)

    qi = proj(_C_QI, _C_KI)
    for p in range(IDX_HEADS // 2):
        qi_ref[:, p * LANES:(p + 1) * LANES] = rope(qi[:, p * LANES:(p + 1) * LANES]).astype(BF16)
    kiw = proj(_C_KI, _C_GSB)
    kiw_ref[...] = kiw
    ki = rope(head_norm(kiw, in_ref[...]))
    if feature_major:
        ki_ref[0] = ki.T[:IDX_DIM, :]
    else:
        ki_ref[...] = ki[:, :IDX_DIM]
    kei_ref[:, :LANES] = ki.astype(BF16)
    kei_ref[:, LANES:] = pltpu.roll(ki, HEAD_DIM, 1).astype(BF16)

    gsb_ref[...] = proj(_C_GSB, _C_GD)
    gd_ref[...] = proj(_C_GD, _C_END)


def _features(x2d, rows_per_pos_cycle, cos, sin, norm_g, w_r, qn, kn, inn, tm, feature_major):
    R = x2d.shape[0]
    n_pos_blocks = rows_per_pos_cycle // tm
    row = lambda i: (i, 0)
    const = lambda i: (0, 0)
    pos = lambda i: (i % n_pos_blocks, 0)
    f32_widths = dict(ksb=256, vsb=256, kd=128, vd=128, ki=64)

    def out_spec(n):
        if feature_major and n in f32_widths:
            return pl.BlockSpec((1, widths[n], tm), lambda i: (i // n_pos_blocks, 0, i % n_pos_blocks))
        return pl.BlockSpec((tm, widths[n]), row)

    def out_shape(n):
        if feature_major and n in f32_widths:
            return jax.ShapeDtypeStruct((R // rows_per_pos_cycle, widths[n], rows_per_pos_cycle), F32)
        return jax.ShapeDtypeStruct((R, widths[n]), dtypes[n])

    names = ["ksb", "vsb", "kd", "vd", "ki", "qsb", "zsb", "qd", "zd", "qi", "kiw", "gsb", "gd",
             "kesb", "vesb", "ked", "kei"]
    widths = dict(f32_widths, qsb=512, zsb=512, qd=512, zd=512, qi=512, kiw=128, gsb=1024, gd=1024,
                  kesb=1024, vesb=1024, ked=512, kei=256)
    dtypes = {n: F32 for n in names}
    for n in ("qsb", "qd", "qi", "kesb", "vesb", "ked", "kei"):
        dtypes[n] = BF16
    vdt_rows = DSA_KV_HEADS * HEAD_DIM
    outs = pl.pallas_call(
        functools.partial(_feat_kernel, feature_major=feature_major),
        grid=(R // tm,),
        in_specs=[
            pl.BlockSpec((tm, D_MODEL), row),
            pl.BlockSpec((1, D_MODEL), const),
            pl.BlockSpec((D_MODEL, _C_END), const),
            pl.BlockSpec((tm, LANES), pos),
            pl.BlockSpec((tm, LANES), pos),
            pl.BlockSpec((1, LANES), const),
            pl.BlockSpec((1, LANES), const),
            pl.BlockSpec((1, LANES), const),
        ],
        out_specs=[out_spec(n) for n in names] + [pl.BlockSpec((1, vdt_rows, tm), lambda i: (i, 0, 0))],
        out_shape=[out_shape(n) for n in names] + [jax.ShapeDtypeStruct((R // tm, vdt_rows, tm), BF16)],
        compiler_params=pltpu.CompilerParams(dimension_semantics=("arbitrary",),
                                             vmem_limit_bytes=VMEM_LIMIT),
        name="features",
    )(x2d, norm_g, w_r, cos, sin, qn, kn, inn)
    return dict(zip(names + ["vdt"], outs))


def _sb_softplus_parts(z):
    e = jnp.exp(-jnp.abs(z))
    return jnp.maximum(z, 0.0) + jnp.log(1.0 + e)


def _sb_kernel(q_ref, ke_ref, ve_ref, o_ref, carry_ref, acc_ref, *, T):
    i = pl.program_id(1)
    rr = lax.broadcasted_iota(I32, (T, T), 0)
    cc = lax.broadcasted_iota(I32, (T, T), 1)
    suffix = (rr > cc).astype(BF16)
    suffix2 = jnp.concatenate([suffix, suffix], axis=0)
    allowed1 = cc < rr
    allowed = jnp.concatenate([allowed1, allowed1], axis=0)

    def widen(v):
        return v if T == LANES else jnp.concatenate([v] * (T // LANES), axis=1)

    def block(j, diag):
        start = pl.multiple_of(j * T, T)
        heads = range(SB_KV_HEADS)
        zs = []
        for kvh in heads:
            q2 = q_ref[:, kvh * LANES:(kvh + 1) * LANES]
            c0 = kvh * 2 * LANES
            ka = ke_ref[pl.ds(start, T), c0:c0 + LANES]
            kb = ke_ref[pl.ds(start, T), c0 + LANES:c0 + 2 * LANES]
            zs.append(jnp.concatenate([_dot_nt(q2, ka), _dot_nt(q2, kb)], axis=0))
        l1ms = [-_sb_softplus_parts(z) for z in zs]
        if diag:
            l1ms = [jnp.where(allowed, l1m, 0.0) for l1m in l1ms]
        tails = [_dot(jnp.concatenate(_split_bf16(l1m), axis=1), suffix2) for l1m in l1ms]
        tots = [jnp.sum(l1m, axis=1, keepdims=True) for l1m in l1ms]
        live = None
        for kvh in heads:
            c0 = kvh * 2 * LANES
            if diag:
                logit = zs[kvh] + l1ms[kvh] + tails[kvh]
                a = jnp.where(allowed, jnp.exp(logit), 0.0).astype(BF16)
                carry = jnp.broadcast_to(tots[kvh], (2 * T, LANES))
            else:
                logit = zs[kvh] + l1ms[kvh] + tails[kvh] + widen(carry_ref[kvh])
                a = jnp.exp(logit).astype(BF16)
                carry = carry_ref[kvh] + tots[kvh]
            va = ve_ref[pl.ds(start, T), c0:c0 + LANES]
            vb = ve_ref[pl.ds(start, T), c0 + LANES:c0 + 2 * LANES]
            o = _dot(a[:T], va) + _dot(a[T:], vb)
            acc_ref[kvh] = o if diag else acc_ref[kvh] + o
            carry_ref[kvh] = carry
            live = carry if live is None else jnp.maximum(live, carry)
        return jnp.max(live)

    def cond(s):
        return jnp.logical_and(s[0] >= 0, s[1] >= SB_DEAD)

    def body(s):
        return s[0] - 1, block(s[0], False)

    lax.while_loop(cond, body, (i - 1, block(i, True)))
    for kvh in range(SB_KV_HEADS):
        o_ref[:, kvh * LANES:(kvh + 1) * LANES] = acc_ref[kvh]


def _sb_attention(qsb, kesb, vesb, B, S, T):
    nq = S // T
    return pl.pallas_call(
        functools.partial(_sb_kernel, T=T),
        grid=(B, nq),
        in_specs=[
            pl.BlockSpec((T, 512), lambda b, i: (b * nq + i, 0)),
            pl.BlockSpec((S, 1024), lambda b, i: (b, 0)),
            pl.BlockSpec((S, 1024), lambda b, i: (b, 0)),
        ],
        out_specs=pl.BlockSpec((T, 512), lambda b, i: (b * nq + i, 0)),
        out_shape=jax.ShapeDtypeStruct((B * S, 512), F32),
        scratch_shapes=[pltpu.VMEM((SB_KV_HEADS, 2 * T, LANES), F32),
                        pltpu.VMEM((SB_KV_HEADS, T, LANES), F32)],
        compiler_params=pltpu.CompilerParams(dimension_semantics=("arbitrary", "arbitrary"),
                                             vmem_limit_bytes=VMEM_LIMIT),
        name="sb_prompt",
    )(qsb, kesb, vesb)


NEG_INF = float("-inf")
COUNT_WIDE = 2


def _canonical_score(score):
    return jnp.where(score == 0.0, 0.0, score)


def _pattern_to_float(pat):
    return pltpu.bitcast(pat ^ ((pat >> 31) & 0x7FFFFFFF), F32)


def _count_rows(score_ref, n_steps, rows, pred):
    width = score_ref.shape[1]

    def body(j, acc):
        start = pl.multiple_of(j * rows, rows)
        scores = score_ref[pl.ds(start, rows), :]
        idx = start + lax.broadcasted_iota(I32, (rows, width), 0)
        hit = jnp.where(pred(scores, idx), 1, 0).astype(I32)
        return acc + hit.reshape(rows // 8, 8, width).sum(axis=0)

    acc = lax.fori_loop(0, n_steps, body, jnp.zeros((8, width), I32))
    return acc.sum(axis=0, keepdims=True)


def _kth_largest_score(score_ref, n_chunks, chunk, topk):
    width = score_ref.shape[1]
    n_steps = (n_chunks + COUNT_WIDE - 1) // COUNT_WIDE
    rows = COUNT_WIDE * chunk

    def bit_step(k, pat):
        cand = pat ^ jnp.left_shift(jnp.int32(1), 31 - k)
        cand_f = _pattern_to_float(cand)
        cnt = _count_rows(score_ref, n_steps, rows, lambda s, idx: s >= cand_f)
        return jnp.where(cnt >= topk, cand, pat)

    pat = lax.fori_loop(0, 32, bit_step, jnp.full((1, width), INT_MIN, I32))
    few = pat == INT_MIN
    t = jnp.where(few, NEG_INF, _pattern_to_float(pat))
    n_gt = _count_rows(score_ref, n_steps, rows, lambda s, idx: s > t)
    return t, jnp.where(few, 0, topk - n_gt)


def _tie_cutoff(score_ref, n_chunks, chunk, t, need, idx_bits):
    width = score_ref.shape[1]
    n_steps = (n_chunks + COUNT_WIDE - 1) // COUNT_WIDE
    rows = COUNT_WIDE * chunk

    def idx_step(k, c):
        cand = c + jnp.left_shift(jnp.int32(1), idx_bits - 1 - k)
        cnt = _count_rows(score_ref, n_steps, rows,
                          lambda s, idx: jnp.logical_and(s == t, idx < cand))
        return jnp.where(cnt < need, cand, c)

    c = lax.fori_loop(0, idx_bits, idx_step, jnp.zeros((1, width), I32))
    return jnp.where(need > 0, c, -1)


def _dsa_kernel(qi_ref, kiw_ref, kei_ref, qd_ref, ked_ref, vdt_ref, o_ref,
                score_ref, acc_ref, kmax_ref, *, T, CK, topk):
    i = pl.program_id(1)
    n_full = (i * T) // CK
    n_chunks = n_full + 1
    rr = lax.broadcasted_iota(I32, (CK, T), 0)
    cc = lax.broadcasted_iota(I32, (CK, T), 1)
    w_t = kiw_ref[...].T[IDX_DIM:IDX_DIM + IDX_HEADS, :]

    def score_chunk(j, last):
        start = pl.multiple_of(j * CK, CK)
        ka = kei_ref[pl.ds(start, CK), :LANES]
        kb = kei_ref[pl.ds(start, CK), LANES:]
        prods = []
        for p in range(IDX_HEADS // 2):
            qp = qi_ref[:, p * LANES:(p + 1) * LANES]
            prods += [_dot_nt(ka, qp), _dot_nt(kb, qp)]
        sc = jnp.zeros((CK, T), F32)
        for h, s in enumerate(prods):
            sc = sc + w_t[h:h + 1, :] * jnp.maximum(s, 0.0)
        sc = _canonical_score(sc * IDX_SCALE)
        if last:
            sc = jnp.where(start + rr <= i * T + cc, sc, NEG_INF)
        score_ref[pl.ds(start, CK), :] = sc

    def score_body(j, carry):
        score_chunk(j, False)
        return carry

    lax.fori_loop(0, n_full, score_body, 0)
    score_chunk(n_full, True)

    @pl.when(n_chunks % COUNT_WIDE != 0)
    def _():
        pad_start = pl.multiple_of(n_chunks * CK, CK)
        score_ref[pl.ds(pad_start, CK), :] = jnp.full((CK, T), NEG_INF, F32)

    t, need = _kth_largest_score(score_ref, n_chunks, CK, topk)
    need_f = need.astype(F32)

    acc_ref[...] = jnp.zeros(acc_ref.shape, F32)
    top_rows = lax.broadcasted_iota(I32, (LANES, T), 0) < HEAD_DIM
    r2 = lax.broadcasted_iota(I32, (CK, CK), 0)
    c2 = lax.broadcasted_iota(I32, (CK, CK), 1)
    prefix = (c2 <= r2).astype(BF16)

    def chunk_inputs(j, tied_before):
        start = pl.multiple_of(j * CK, CK)
        sc = score_ref[pl.ds(start, CK), :]
        tied = sc == t
        rank = tied_before + _dot(prefix, jnp.where(tied, 1.0, 0.0).astype(BF16))
        sel = jnp.logical_or(sc > t, jnp.logical_and(tied, rank <= need_f))
        return start, sel, rank[CK - 1:CK, :]

    def pair_inputs(j, start, p):
        kvh = p // 2
        kcol = kvh * 2 * LANES
        qp = qd_ref[:, p * LANES:(p + 1) * LANES]
        vt = vdt_ref[j, kvh * HEAD_DIM:(kvh + 1) * HEAD_DIM, :]
        kblks = [ked_ref[pl.ds(start, CK), kcol + e * LANES:kcol + (e + 1) * LANES] for e in range(2)]
        return qp, vt, kblks

    def attend_online(j, state):
        ms, ls, tied_before = state
        start, sel, tied_before = chunk_inputs(j, tied_before)
        new_ms, new_ls = [], []
        for p in range(DSA_HEADS // 2):
            qp, vt, kblks = pair_inputs(j, start, p)
            outs, alphas = [], []
            for e in range(2):
                h = 2 * p + e
                lg = jnp.where(sel, _dot_nt(kblks[e], qp), NEG_BIG)
                m_new = jnp.maximum(ms[h], jnp.max(lg, axis=0, keepdims=True))
                alpha = jnp.exp(ms[h] - m_new)
                pr = jnp.where(sel, jnp.exp(lg - m_new), 0.0)
                new_ls.append(alpha * ls[h] + jnp.sum(pr, axis=0, keepdims=True))
                new_ms.append(m_new)
                outs.append(_dot(vt, pr.astype(BF16)))
                alphas.append(alpha)
            scale = jnp.where(top_rows, alphas[0], alphas[1])
            acc_ref[p] = acc_ref[p] * scale + jnp.concatenate(outs, axis=0)
        return tuple(new_ms), tuple(new_ls), tied_before

    def softmax_online(_):
        init = (tuple(jnp.full((1, T), NEG_BIG, F32) for _ in range(DSA_HEADS)),
                tuple(jnp.zeros((1, T), F32) for _ in range(DSA_HEADS)),
                jnp.zeros((1, T), F32))
        return lax.fori_loop(0, n_chunks, attend_online, init)[1]

    def attend_unshifted(j, state):
        ls, tied_before = state
        start, sel, tied_before = chunk_inputs(j, tied_before)
        pairs = [pair_inputs(j, start, p) for p in range(DSA_HEADS // 2)]
        lgs = [_dot_nt(kblks[e], qp) for qp, _, kblks in pairs for e in range(2)]
        prs = [jnp.where(sel, jnp.exp(lg), 0.0) for lg in lgs]
        new_ls = tuple(l + pr.reshape(CK // 8, 8, T).sum(axis=0) for l, pr in zip(ls, prs))
        for p, (_, vt, _) in enumerate(pairs):
            outs = [_dot(vt, prs[2 * p + e].astype(BF16)) for e in range(2)]
            acc_ref[p] = acc_ref[p] + jnp.concatenate(outs, axis=0)
        return new_ls, tied_before

    def softmax_unshifted(_):
        init = (tuple(jnp.zeros((8, T), F32) for _ in range(DSA_HEADS)), jnp.zeros((1, T), F32))
        ls = lax.fori_loop(0, n_chunks, attend_unshifted, init)[0]
        return tuple(l.sum(axis=0, keepdims=True) for l in ls)

    @pl.when(i == 0)
    def _():
        for kvh in range(DSA_KV_HEADS):
            def k_norm(c, m):
                rows = ked_ref[pl.ds(pl.multiple_of(c * CK, CK), CK),
                               kvh * 2 * LANES:kvh * 2 * LANES + LANES].astype(F32)
                return jnp.maximum(m, jnp.max(jnp.sum(rows * rows, axis=1, keepdims=True)))
            kmax_ref[kvh] = lax.fori_loop(0, ked_ref.shape[0] // CK, k_norm, jnp.float32(0.0))

    bound2 = jnp.float32(0.0)
    for p in range(DSA_HEADS // 2):
        qf = qd_ref[:, p * LANES:(p + 1) * LANES].astype(F32)
        q2max = jnp.max(jnp.sum(qf * qf, axis=1, keepdims=True))
        bound2 = jnp.maximum(bound2, q2max * kmax_ref[p // 2])
    ls = lax.cond(bound2 <= LOGIT_SAFE * LOGIT_SAFE, softmax_unshifted, softmax_online, 0)
    for p in range(DSA_HEADS // 2):
        denom = jnp.where(top_rows, ls[2 * p], ls[2 * p + 1])
        o_ref[:, p * LANES:(p + 1) * LANES] = (acc_ref[p] / denom).T


def _dsa_attention(qi, kiw, kei, qd, ked, vdt, B, S, T, topk):
    nq = S // T
    CK = vdt.shape[2]
    assert S % (COUNT_WIDE * CK) == 0 and CK % T == 0
    qrow = lambda b, i: (b * nq + i, 0)
    batch = lambda b, i: (b, 0)
    return pl.pallas_call(
        functools.partial(_dsa_kernel, T=T, CK=CK, topk=topk),
        grid=(B, nq),
        in_specs=[
            pl.BlockSpec((T, 512), qrow),
            pl.BlockSpec((T, LANES), qrow),
            pl.BlockSpec((S, 256), batch),
            pl.BlockSpec((T, 512), qrow),
            pl.BlockSpec((S, 512), batch),
            pl.BlockSpec((S // CK, LANES, CK), lambda b, i: (b, 0, 0)),
        ],
        out_specs=pl.BlockSpec((T, 512), qrow),
        out_shape=jax.ShapeDtypeStruct((B * S, 512), F32),
        scratch_shapes=[
            pltpu.VMEM((S, T), F32),
            pltpu.VMEM((DSA_HEADS // 2, LANES, T), F32),
            pltpu.SMEM((DSA_KV_HEADS,), F32),
        ],
        compiler_params=pltpu.CompilerParams(dimension_semantics=("arbitrary", "arbitrary"),
                                             vmem_limit_bytes=VMEM_LIMIT),
        name="dsa_prompt",
    )(qi, kiw, kei, qd, ked, vdt)


def _merge_kernel(x_ref, osb_ref, zsb_ref, od_ref, zd_ref, gsb_ref, gd_ref,
                  wsb_ref, wd_ref, wo_ref, y_ref):
    zsb = zsb_ref[...]
    zd = zd_ref[...]
    a_sb = (osb_ref[...] * (zsb * jax.nn.sigmoid(zsb))).astype(BF16)
    a_d = (od_ref[...] * (zd * jax.nn.sigmoid(zd))).astype(BF16)
    u_sb = _dot(a_sb, wsb_ref[...])
    u_d = _dot(a_d, wd_ref[...])
    mixed = jax.nn.sigmoid(gsb_ref[...]) * u_sb + jax.nn.sigmoid(gd_ref[...]) * u_d
    y_ref[...] = x_ref[...] + _dot(mixed.astype(BF16), wo_ref[...])


def _merge(x2d, osb, zsb, od, zd, gsb, gd, wsb, wd, wo, tm):
    R = x2d.shape[0]
    row = lambda i: (i, 0)
    const = lambda i: (0, 0)
    return pl.pallas_call(
        _merge_kernel,
        grid=(R // tm,),
        in_specs=[
            pl.BlockSpec((tm, D_MODEL), row),
            pl.BlockSpec((tm, 512), row),
            pl.BlockSpec((tm, 512), row),
            pl.BlockSpec((tm, 512), row),
            pl.BlockSpec((tm, 512), row),
            pl.BlockSpec((tm, D_MODEL), row),
            pl.BlockSpec((tm, D_MODEL), row),
            pl.BlockSpec((512, D_MODEL), const),
            pl.BlockSpec((512, D_MODEL), const),
            pl.BlockSpec((D_MODEL, D_MODEL), const),
        ],
        out_specs=pl.BlockSpec((tm, D_MODEL), row),
        out_shape=jax.ShapeDtypeStruct((R, D_MODEL), F32),
        compiler_params=pltpu.CompilerParams(dimension_semantics=("arbitrary",),
                                             vmem_limit_bytes=VMEM_LIMIT),
        name="merge",
    )(x2d, osb, zsb, od, zd, gsb, gd, wsb, wd, wo)


def _rope_tables(pos):
    inv = ROPE_THETA ** (-jnp.arange(HALF, dtype=F32) / HALF)
    ang = pos.astype(F32)[:, None] * inv[None, :]
    cos = jnp.cos(ang)
    sin = jnp.sin(ang)
    return jnp.tile(cos, (1, 4)), jnp.tile(jnp.concatenate([-sin, sin], axis=1), (1, 2))


def _prep_weights(norm_g, w_in, q_norm_g, k_norm_g, idx_k_norm_g, w_up_sb, w_up_dsa, w_out):
    pad = jnp.zeros((D_MODEL, _C_GSB - _C_KI - (_W_SPLIT - _C_KI)), w_in.dtype)
    w_r = jnp.concatenate([w_in[:, :_W_SPLIT], pad, w_in[:, _W_SPLIT:]], axis=1).astype(BF16)
    qn = jnp.tile(q_norm_g, 2)[None, :]
    kn = jnp.tile(k_norm_g, 2)[None, :]
    inn = jnp.concatenate([idx_k_norm_g, jnp.zeros((LANES - IDX_DIM,), F32)])[None, :]
    return (norm_g[None, :], w_r, qn, kn, inn,
            w_up_sb.astype(BF16), w_up_dsa.astype(BF16), w_out.astype(BF16))


def _prompt_layer(x, weights, t_feat, t_attn):
    B, S, _ = x.shape
    norm_g, w_r, qn, kn, inn, wsb, wd, wo = weights
    x2d = x.reshape(B * S, D_MODEL)
    cos, sin = _rope_tables(jnp.arange(S))
    f = _features(x2d, S, cos, sin, norm_g, w_r, qn, kn, inn, t_feat, True)
    topk = min(DSA_TOPK_MAX, S // 4)
    osb = _sb_attention(f["qsb"], f["kesb"], f["vesb"], B, S, t_attn)
    od = _dsa_attention(f["qi"], f["kiw"], f["kei"], f["qd"], f["ked"], f["vdt"], B, S, t_feat, topk)
    y = _merge(x2d, osb, f["zsb"], od, f["zd"], f["gsb"], f["gd"], wsb, wd, wo, t_feat)

    def rows(a, heads):
        return jnp.moveaxis(a.reshape(B, heads, HEAD_DIM, S), 3, 1)[None]

    return (y.reshape(B, S, D_MODEL),
            rows(f["ksb"], SB_KV_HEADS), rows(f["vsb"], SB_KV_HEADS),
            rows(f["kd"], DSA_KV_HEADS), rows(f["vd"], DSA_KV_HEADS),
            jnp.moveaxis(f["ki"], 2, 1)[None])


PAGE_UNROLL = 32


def _page_copy(cache_hbm, pt_ref, buf, sems, ci, seq, slot, p):
    return pltpu.make_async_copy(cache_hbm.at[pt_ref[seq, p]], buf.at[slot, p], sems.at[ci, slot])


def _stream_pages(caches, pt_ref, bufs, sems, n_pages):
    b = pl.program_id(0)
    nb = pl.num_programs(0)
    slot = b % 2

    def start_all_pages(seq, slot_):
        def body(p, carry):
            for ci, (cache, buf) in enumerate(zip(caches, bufs)):
                _page_copy(cache, pt_ref, buf, sems, ci, seq, slot_, p).start()
            return carry
        lax.fori_loop(0, n_pages, body, 0, unroll=min(8, n_pages))

    @pl.when(b == 0)
    def _():
        start_all_pages(0, 0)

    @pl.when(b + 1 < nb)
    def _():
        start_all_pages(b + 1, 1 - slot)

    for p in range(n_pages):
        for ci, (cache, buf) in enumerate(zip(caches, bufs)):
            _page_copy(cache, pt_ref, buf, sems, ci, b, slot, p).wait()
    return slot


def _dec_sb_kernel(pt_ref, qm_ref, k_hbm, v_hbm, o_ref, kfirst, vfirst, kmore, vmore, sems, *, n_pages, G):
    b = pl.program_id(0)
    nb = pl.num_programs(0)
    slot = b % 2
    n_groups = n_pages // G
    qm = qm_ref[0]
    P = PAGE_SIZE
    rr = lax.broadcasted_iota(I32, (P, P), 0)
    cc = lax.broadcasted_iota(I32, (P, P), 1)
    suffix = (rr > cc).astype(BF16)
    ones = jnp.ones((P, LANES), BF16)

    def group_copies(seq, g, kdst, vdst, ksem, vsem):
        copies = []
        for r in range(G):
            page = pt_ref[seq, n_pages - (g + 1) * G + r]
            copies.append(pltpu.make_async_copy(k_hbm.at[page], kdst.at[r], ksem))
            copies.append(pltpu.make_async_copy(v_hbm.at[page], vdst.at[r], vsem))
        return copies

    def first_copies(seq, s):
        return group_copies(seq, 0, kfirst.at[s], vfirst.at[s], sems.at[0, s], sems.at[1, s])

    def process(kb, vb, carry, acc):
        zs, l1ms, tails, tots = [], [], [], []
        for r in range(G):
            z = _dot(qm, kb[r].astype(BF16))
            l1m = -_sb_softplus_parts(z)
            hi, lo = _split_bf16(l1m)
            zs.append(z)
            l1ms.append(l1m)
            tails.append(_dot(hi, suffix) + _dot(lo, suffix))
            tots.append(_dot(hi, ones) + _dot(lo, ones))
        for r in reversed(range(G)):
            a = jnp.exp(zs[r] + l1ms[r] + tails[r] + carry).astype(BF16)
            acc = acc + _dot_nt(a, vb[r].astype(BF16))
            carry = carry + tots[r]
        return carry, acc

    @pl.when(b == 0)
    def _():
        for c in first_copies(0, 0):
            c.start()

    @pl.when(b + 1 < nb)
    def _():
        for c in first_copies(b + 1, 1 - slot):
            c.start()

    for c in first_copies(b, slot):
        c.wait()
    n_heads = qm.shape[0]
    carry, acc = process(kfirst.at[slot], vfirst.at[slot],
                         jnp.zeros((n_heads, LANES), F32), jnp.zeros((n_heads, qm.shape[1]), F32))

    def cond(s):
        return jnp.logical_and(s[0] < n_groups, s[3] >= SB_DEAD)

    def body(s):
        g, carry, acc, _ = s
        copies = group_copies(b, g, kmore, vmore, sems.at[2, 0], sems.at[2, 1])
        for c in copies:
            c.start()
        for c in copies:
            c.wait()
        carry, acc = process(kmore, vmore, carry, acc)
        return g + 1, carry, acc, jnp.max(carry)

    _, _, acc, _ = lax.while_loop(cond, body, (jnp.int32(1), carry, acc, jnp.max(carry)))
    o_ref[0] = acc


def _dec_sb(page_table, qm, cache_k, cache_v, group):
    DB, n_pages = page_table.shape
    width = cache_k.shape[1]
    return pl.pallas_call(
        functools.partial(_dec_sb_kernel, n_pages=n_pages, G=group),
        grid_spec=pltpu.PrefetchScalarGridSpec(
            num_scalar_prefetch=1,
            grid=(DB,),
            in_specs=[
                pl.BlockSpec((1, SB_HEADS, width), lambda b, pt: (b, 0, 0)),
                pl.BlockSpec(memory_space=pl.ANY),
                pl.BlockSpec(memory_space=pl.ANY),
            ],
            out_specs=pl.BlockSpec((1, SB_HEADS, width), lambda b, pt: (b, 0, 0)),
            scratch_shapes=[
                pltpu.VMEM((2, group, width, PAGE_SIZE), F32),
                pltpu.VMEM((2, group, width, PAGE_SIZE), F32),
                pltpu.VMEM((group, width, PAGE_SIZE), F32),
                pltpu.VMEM((group, width, PAGE_SIZE), F32),
                pltpu.SemaphoreType.DMA((3, 2)),
            ]),
        out_shape=jax.ShapeDtypeStruct((DB, SB_HEADS, width), F32),
        compiler_params=pltpu.CompilerParams(dimension_semantics=("arbitrary",),
                                             vmem_limit_bytes=VMEM_LIMIT),
        name="dec_sb",
    )(page_table, qm, cache_k, cache_v)


def _dec_idx_kernel(pt_ref, qi_ref, w_ref, kinew_ref, k_hbm, score_ref, kbuf, sems, *, n_pages, n_rows):
    slot = _stream_pages((k_hbm,), pt_ref, (kbuf,), sems, n_pages)
    qi = qi_ref[0]
    w = w_ref[0]

    def score(pp, carry):
        kk = jnp.concatenate([kbuf[slot, 2 * pp], kbuf[slot, 2 * pp + 1]], axis=1).astype(BF16)
        s = _dot(qi, kk)
        sc = _canonical_score(jnp.sum(w * jnp.maximum(s, 0.0), axis=0, keepdims=True) * IDX_SCALE)
        score_ref[0, pl.ds(2 * pp, 1), :] = sc[:, :PAGE_SIZE]
        score_ref[0, pl.ds(2 * pp + 1, 1), :] = sc[:, PAGE_SIZE:]
        return carry

    lax.fori_loop(0, n_pages // 2, score, 0, unroll=min(PAGE_UNROLL // 2, n_pages // 2))
    s_new = jnp.sum(qi.astype(F32) * kinew_ref[0].astype(F32), axis=1, keepdims=True)
    sc_new = jnp.sum(w * jnp.maximum(s_new, 0.0), axis=0, keepdims=True) * IDX_SCALE
    tail_rows = n_rows - n_pages
    r = lax.broadcasted_iota(I32, (tail_rows, LANES), 0)
    c = lax.broadcasted_iota(I32, (tail_rows, LANES), 1)
    new_row = jnp.broadcast_to(_canonical_score(sc_new), (tail_rows, LANES))
    score_ref[0, n_pages:, :] = jnp.where(jnp.logical_and(r == 0, c == 0), new_row, NEG_INF)


def _dec_idx(page_table, qi, w, kinew, cache_k, n_rows):
    DB, n_pages = page_table.shape
    per_seq = lambda b, pt: (b, 0, 0)
    return pl.pallas_call(
        functools.partial(_dec_idx_kernel, n_pages=n_pages, n_rows=n_rows),
        grid_spec=pltpu.PrefetchScalarGridSpec(
            num_scalar_prefetch=1,
            grid=(DB,),
            in_specs=[
                pl.BlockSpec((1, IDX_HEADS, IDX_DIM), per_seq),
                pl.BlockSpec((1, IDX_HEADS, 1), per_seq),
                pl.BlockSpec((1, 1, IDX_DIM), per_seq),
                pl.BlockSpec(memory_space=pl.ANY),
            ],
            out_specs=pl.BlockSpec((1, n_rows, LANES), per_seq),
            scratch_shapes=[
                pltpu.VMEM((2, n_pages, IDX_DIM, PAGE_SIZE), F32),
                pltpu.SemaphoreType.DMA((1, 2)),
            ]),
        out_shape=jax.ShapeDtypeStruct((DB, n_rows, LANES), F32),
        compiler_params=pltpu.CompilerParams(dimension_semantics=("arbitrary",),
                                             vmem_limit_bytes=VMEM_LIMIT),
        name="dec_idx",
    )(page_table, qi, w, kinew, cache_k)


def _dec_thr_kernel(scores_ref, t_ref, c_ref, score_scr, *, n_rows, topk, idx_bits):
    for p in range(n_rows):
        score_scr[p * LANES:(p + 1) * LANES, :] = scores_ref[:, p * LANES:(p + 1) * LANES].T
    n_chunks, chunk = n_rows // 2, 2 * LANES
    t, need = _kth_largest_score(score_scr, n_chunks, chunk, topk)
    t_ref[...] = t
    c_ref[...] = _tie_cutoff(score_scr, n_chunks, chunk, t, need, idx_bits)


def _dec_thr(scores2d, n_rows, topk):
    DB = scores2d.shape[0]
    assert n_rows % (2 * COUNT_WIDE) == 0
    idx_bits = (n_rows * LANES - 1).bit_length()
    return pl.pallas_call(
        functools.partial(_dec_thr_kernel, n_rows=n_rows, topk=topk, idx_bits=idx_bits),
        out_shape=[jax.ShapeDtypeStruct((1, DB), F32), jax.ShapeDtypeStruct((1, DB), I32)],
        scratch_shapes=[pltpu.VMEM((n_rows * LANES, DB), F32)],
        compiler_params=pltpu.CompilerParams(vmem_limit_bytes=VMEM_LIMIT),
        name="dec_thr",
    )(scores2d)


def _dec_dsa_kernel(pt_ref, c_ref, t_ref, qm_ref, scores_ref, kdnew_ref, vdnew_ref, k_hbm, v_hbm, o_ref,
                    kbuf, vbuf, lbuf, sems, *, n_pages):
    b = pl.program_id(0)
    slot = _stream_pages((k_hbm, v_hbm), pt_ref, (kbuf, vbuf), sems, n_pages)
    qm = qm_ref[0]
    t = t_ref[0]
    cidx = c_ref[b]
    lane = lax.broadcasted_iota(I32, (1, LANES), 1)

    def selected(p):
        sc = scores_ref[0, pl.ds(p, 1), :]
        idx = p * PAGE_SIZE + lane
        return jnp.logical_or(sc > t, jnp.logical_and(sc == t, idx <= cidx))

    def page_pair(buf, pp):
        return jnp.concatenate([buf[slot, 2 * pp], buf[slot, 2 * pp + 1]], axis=1).astype(BF16)

    def logits(pp, m):
        sel = jnp.concatenate([selected(2 * pp), selected(2 * pp + 1)], axis=1)
        lg = jnp.where(sel, _dot(qm, page_pair(kbuf, pp)), NEG_BIG)
        lbuf[pp] = lg
        return jnp.maximum(m, lg)

    n_heads = qm.shape[0]
    n_pairs = n_pages // 2
    unroll = min(PAGE_UNROLL // 2, n_pairs)
    m = lax.fori_loop(0, n_pairs, logits, jnp.full((n_heads, 2 * LANES), NEG_BIG, F32), unroll=unroll)
    sel_new = selected(n_pages)[:, :1]
    lg_new = jnp.sum(qm.astype(F32) * kdnew_ref[0].astype(F32), axis=1, keepdims=True)
    lg_new = jnp.where(sel_new, lg_new, NEG_BIG)
    m = jnp.maximum(jnp.max(m, axis=1, keepdims=True), lg_new)

    def attend(pp, state):
        l, acc = state
        pr = jnp.exp(lbuf[pp] - m)
        return l + pr, acc + _dot_nt(pr.astype(BF16), page_pair(vbuf, pp))

    l, acc = lax.fori_loop(0, n_pairs, attend,
                           (jnp.zeros((n_heads, 2 * LANES), F32), jnp.zeros((n_heads, LANES), F32)),
                           unroll=unroll)
    pr_new = jnp.where(sel_new, jnp.exp(lg_new - m), 0.0)
    pr_new_b = pr_new.astype(BF16).astype(F32)
    acc = acc + pr_new_b * vdnew_ref[0].astype(F32)
    denom = jnp.sum(l, axis=1, keepdims=True) + pr_new
    o_ref[0] = acc / denom


def _dec_dsa(page_table, cidx, t_rep, qm, scores, kdnew, vdnew, cache_k, cache_v):
    DB, n_pages = page_table.shape
    n_rows = scores.shape[1]
    per_seq = lambda b, pt, c: (b, 0, 0)
    return pl.pallas_call(
        functools.partial(_dec_dsa_kernel, n_pages=n_pages),
        grid_spec=pltpu.PrefetchScalarGridSpec(
            num_scalar_prefetch=2,
            grid=(DB,),
            in_specs=[
                pl.BlockSpec((1, 1, LANES), per_seq),
                pl.BlockSpec((1, DSA_HEADS, LANES), per_seq),
                pl.BlockSpec((1, n_rows, LANES), per_seq),
                pl.BlockSpec((1, 1, LANES), per_seq),
                pl.BlockSpec((1, 1, LANES), per_seq),
                pl.BlockSpec(memory_space=pl.ANY),
                pl.BlockSpec(memory_space=pl.ANY),
            ],
            out_specs=pl.BlockSpec((1, DSA_HEADS, LANES), per_seq),
            scratch_shapes=[
                pltpu.VMEM((2, n_pages, PAGE_SIZE, LANES), F32),
                pltpu.VMEM((2, n_pages, PAGE_SIZE, LANES), F32),
                pltpu.VMEM((n_pages // 2, DSA_HEADS, 2 * LANES), F32),
                pltpu.SemaphoreType.DMA((2, 2)),
            ]),
        out_shape=jax.ShapeDtypeStruct((DB, DSA_HEADS, LANES), F32),
        compiler_params=pltpu.CompilerParams(dimension_semantics=("arbitrary",),
                                             vmem_limit_bytes=VMEM_LIMIT),
        name="dec_dsa",
    )(page_table, cidx, t_rep, qm, scores, kdnew, vdnew, cache_k, cache_v)


def _heads_in_kv_lanes(q, n_heads, n_kv):
    R = q.shape[0]
    q3 = q.reshape(R, n_heads, 1, HEAD_DIM)
    kv_of_head = jnp.arange(n_heads) // (n_heads // n_kv)
    onehot = (kv_of_head[:, None] == jnp.arange(n_kv)[None, :])[None, :, :, None]
    return jnp.where(onehot, q3, jnp.zeros((), q.dtype)).reshape(R, n_heads, n_kv * HEAD_DIM)


def _own_kv_lanes(o, n_heads, n_kv):
    R = o.shape[0]
    o4 = o.reshape(R, n_heads, n_kv, HEAD_DIM)
    group = n_heads // n_kv
    return jnp.concatenate([o4[:, h, h // group, :] for h in range(n_heads)], axis=1)


def _sample_layer(x, caches, page_table, weights):
    DB, T, _ = x.shape
    c_sb_k, c_sb_v, c_dsa_k, c_dsa_v, c_idx_k = caches
    norm_g, w_r, qn, kn, inn, wsb, wd, wo = weights
    n_pages = page_table.shape[1]
    past = n_pages * PAGE_SIZE
    x2d = x.reshape(DB * T, D_MODEL)
    cos, sin = _rope_tables(jnp.full((DB * T,), past))
    f = _features(x2d, DB * T, cos, sin, norm_g, w_r, qn, kn, inn, DB * T, False)

    n_pool = c_sb_k.shape[0]

    def pages_t(c):
        return jnp.moveaxis(c, 1, -1).reshape(n_pool, -1, PAGE_SIZE)

    qm_sb = _heads_in_kv_lanes(f["qsb"], SB_HEADS, SB_KV_HEADS)
    o_sb = _dec_sb(page_table, qm_sb, pages_t(c_sb_k), pages_t(c_sb_v), 2)
    osb = _own_kv_lanes(o_sb, SB_HEADS, SB_KV_HEADS)

    n_rows = n_pages + 8
    qi3 = f["qi"].reshape(DB, IDX_HEADS, IDX_DIM)
    w3 = f["kiw"][:, IDX_DIM:IDX_DIM + IDX_HEADS].reshape(DB, IDX_HEADS, 1)
    kinew = f["kei"][:, :IDX_DIM].reshape(DB, 1, IDX_DIM)
    scores = _dec_idx(page_table, qi3, w3, kinew, pages_t(c_idx_k), n_rows)
    topk = min(DSA_TOPK_MAX, (past + T) // 4)
    t, cidx = _dec_thr(scores.reshape(DB, n_rows * LANES), n_rows, topk)
    t_rep = jnp.broadcast_to(t.reshape(DB, 1, 1), (DB, 1, LANES))

    qm_d = _heads_in_kv_lanes(f["qd"], DSA_HEADS, DSA_KV_HEADS)
    kdnew = f["kd"].astype(BF16).reshape(DB, 1, LANES)
    vdnew = f["vd"].astype(BF16).reshape(DB, 1, LANES)
    o_d = _dec_dsa(page_table, cidx.reshape(DB), t_rep, qm_d, scores, kdnew, vdnew,
                   pages_t(c_dsa_k), pages_t(c_dsa_v))
    od = _own_kv_lanes(o_d, DSA_HEADS, DSA_KV_HEADS)

    y = _merge(x2d, osb, f["zsb"], od, f["zd"], f["gsb"], f["gd"], wsb, wd, wo, DB * T)
    return (y.reshape(DB, T, D_MODEL),
            f["ksb"].reshape(1, DB, T, SB_KV_HEADS, HEAD_DIM),
            f["vsb"].reshape(1, DB, T, SB_KV_HEADS, HEAD_DIM),
            f["kd"].reshape(1, DB, T, DSA_KV_HEADS, HEAD_DIM),
            f["vd"].reshape(1, DB, T, DSA_KV_HEADS, HEAD_DIM),
            f["ki"].reshape(1, DB, T, IDX_DIM))


def kernel(x_prompt, x_sample, cache_sb_k, cache_sb_v, cache_dsa_k, cache_dsa_v, cache_idx_k, page_table,
           norm_g, w_in, q_norm_g, k_norm_g, idx_k_norm_g, w_up_sb, w_up_dsa, w_out):
    assert norm_g.shape[0] == 1 and x_sample.shape[1] == 1
    weights = _prep_weights(norm_g[0], w_in[0], q_norm_g[0], k_norm_g[0], idx_k_norm_g[0],
                            w_up_sb[0], w_up_dsa[0], w_out[0])
    p = _prompt_layer(x_prompt, weights, 256, 128)
    caches = (cache_sb_k[0], cache_sb_v[0], cache_dsa_k[0], cache_dsa_v[0], cache_idx_k[0])
    s = _sample_layer(x_sample, caches, page_table, weights)
    return (p[0], s[0], p[1], p[2], p[3], p[4], p[5], s[1], s[2], s[3], s[4], s[5])
```

```python
import functools

import jax
import jax.numpy as jnp
from jax import lax
from jax.experimental import pallas as pl
from jax.experimental.pallas import tpu as pltpu

F32 = jnp.float32
BF16 = jnp.bfloat16
I32 = jnp.int32
I16 = jnp.int16

LANES = 128
HEAD_DIM = 64
HALF = HEAD_DIM // 2
SB_HEADS, SB_KV_HEADS = 8, 4
DSA_HEADS, DSA_KV_HEADS = 8, 2
IDX_HEADS, IDX_DIM = 8, 64
D_MODEL = 1024
PAGE_SIZE = 128
DSA_TOPK_MAX = 256
ROPE_THETA = 10000.0
NORM_EPS = 1e-6
ATTN_SCALE = HEAD_DIM ** -0.5
IDX_SCALE = (IDX_HEADS * IDX_DIM) ** -0.5
INT_MIN = -2 ** 31
NEG_BIG = -1e30
SB_DEAD = -110.0
LOGIT_SAFE = 40.0
VMEM_LIMIT = 56 * 1024 * 1024
PROMPT_TILES = (256, 256, 128, 256, 256)

_C_QSB, _C_KSB, _C_VSB, _C_ZSB = 0, 512, 768, 1024
_C_QD, _C_KD, _C_VD, _C_ZD = 1536, 2048, 2176, 2304
_C_QI, _C_KI, _C_GSB, _C_GD, _C_END = 2816, 3328, 3456, 4480, 5504
_W_SPLIT = 3400


def _dot(a, b):
    return jnp.dot(a, b, preferred_element_type=F32)


def _dot_nt(a, b):
    return lax.dot_general(a, b, (((1,), (1,)), ((), ())), preferred_element_type=F32)


def _dot_tn(a, b):
    return lax.dot_general(a, b, (((0,), (0,)), ((), ())), preferred_element_type=F32)


def _split_bf16(x):
    hi = x.astype(BF16)
    lo = (x - hi.astype(F32)).astype(BF16)
    return hi, lo


def _feat_kernel(x_ref, g_ref, w_ref, cos_ref, sin_ref, qn_ref, kn_ref, in_ref,
                 ksb_ref, vsb_ref, kd_ref, vd_ref, ki_ref,
                 qsb_ref, zsb_ref, qd_ref, zd_ref, qi_ref, kiw_ref, gsb_ref, gd_ref,
                 kesb_ref, vesb_ref, ked_ref, kei_ref, vdt_ref, *, feature_major):
    def put_rows(ref, val):
        if feature_major:
            ref[0] = val.T
        else:
            ref[...] = val

    x = x_ref[...]
    ms = jnp.mean(x * x, axis=-1, keepdims=True)
    xn = (x * lax.rsqrt(ms + NORM_EPS) * g_ref[...]).astype(BF16)

    tm = x.shape[0]
    lane = lax.broadcasted_iota(I32, (tm, LANES), 1)
    lo_half = lane < HEAD_DIM
    lo_rot = (lane % HEAD_DIM) < HALF
    r = lax.broadcasted_iota(I32, (LANES, LANES), 0) // HEAD_DIM
    c = lax.broadcasted_iota(I32, (LANES, LANES), 1) // HEAD_DIM
    group_ones = (r == c).astype(BF16)
    group_ones2 = jnp.concatenate([group_ones, group_ones], axis=0)
    cos = cos_ref[...]
    sin = sin_ref[...]

    def proj(c0, c1):
        return _dot(xn, w_ref[:, c0:c1])

    def head_norm(v, gain):
        ss = _dot(jnp.concatenate(_split_bf16(v * v), axis=1), group_ones2)
        return v * lax.rsqrt(ss * (1.0 / HEAD_DIM) + NORM_EPS) * gain

    def rope(v):
        rot = jnp.where(lo_rot, pltpu.roll(v, LANES - HALF, 1), pltpu.roll(v, HALF, 1))
        return v * cos + rot * sin

    def expand(pair, e):
        if e == 0:
            a = jnp.where(lo_half, pair, 0.0)
            return a, pltpu.roll(a, HEAD_DIM, 1)
        b = jnp.where(lo_half, 0.0, pair)
        return pltpu.roll(b, HEAD_DIM, 1), b

    def store_expanded(dst_ref, val, n_heads):
        for h in range(n_heads):
            pair = val[:, (h // 2) * LANES:(h // 2 + 1) * LANES]
            a, b = expand(pair, h % 2)
            dst_ref[:, h * 2 * LANES:h * 2 * LANES + LANES] = a.astype(BF16)
            dst_ref[:, h * 2 * LANES + LANES:(h + 1) * 2 * LANES] = b.astype(BF16)

    qsb_ref[...] = (proj(_C_QSB, _C_KSB) * ATTN_SCALE).astype(BF16)
    ksb = proj(_C_KSB, _C_VSB)
    put_rows(ksb_ref, ksb)
    store_expanded(kesb_ref, ksb, SB_KV_HEADS)
    vsb = proj(_C_VSB, _C_ZSB)
    put_rows(vsb_ref, vsb)
    store_expanded(vesb_ref, vsb, SB_KV_HEADS)
    zsb_ref[...] = proj(_C_ZSB, _C_QD)

    qd = proj(_C_QD, _C_KD)
    for p in range(DSA_HEADS // 2):
        blk = rope(head_norm(qd[:, p * LANES:(p + 1) * LANES], qn_ref[...]))
        qd_ref[:, p * LANES:(p + 1) * LANES] = (blk * ATTN_SCALE).astype(BF16)
    kd = rope(head_norm(proj(_C_KD, _C_VD), kn_ref[...]))
    put_rows(kd_ref, kd)
    store_expanded(ked_ref, kd, DSA_KV_HEADS)
    vd = proj(_C_VD, _C_ZD)
    put_rows(vd_ref, vd)
    ck = vdt_ref.shape[2]
    for c in range(vdt_ref.shape[0]):
        vdt_ref[c] = vd[c * ck:(c + 1) * ck, :].T.astype(BF16)
    zd_ref[...] = proj(_C_ZD, _C_QI)

    qi = proj(_C_QI, _C_KI)
    for p in range(IDX_HEADS // 2):
        qi_ref[:, p * LANES:(p + 1) * LANES] = rope(qi[:, p * LANES:(p + 1) * LANES]).astype(BF16)
    kiw = proj(_C_KI, _C_GSB)
    kiw_ref[...] = kiw
    ki = rope(head_norm(kiw, in_ref[...]))
    if feature_major:
        ki_ref[0] = ki.T[:IDX_DIM, :]
    else:
        ki_ref[...] = ki[:, :IDX_DIM]
    kei_ref[:, :LANES] = ki.astype(BF16)
    kei_ref[:, LANES:] = pltpu.roll(ki, HEAD_DIM, 1).astype(BF16)

    gsb_ref[...] = proj(_C_GSB, _C_GD)
    gd_ref[...] = proj(_C_GD, _C_END)


def _features(x2d, rows_per_pos_cycle, cos, sin, norm_g, w_r, qn, kn, inn, tm, key_chunk, feature_major):
    R = x2d.shape[0]
    assert tm % key_chunk == 0
    n_pos_blocks = rows_per_pos_cycle // tm
    row = lambda i: (i, 0)
    const = lambda i: (0, 0)
    pos = lambda i: (i % n_pos_blocks, 0)
    f32_widths = dict(ksb=256, vsb=256, kd=128, vd=128, ki=64)

    def out_spec(n):
        if feature_major and n in f32_widths:
            return pl.BlockSpec((1, widths[n], tm), lambda i: (i // n_pos_blocks, 0, i % n_pos_blocks))
        return pl.BlockSpec((tm, widths[n]), row)

    def out_shape(n):
        if feature_major and n in f32_widths:
            return jax.ShapeDtypeStruct((R // rows_per_pos_cycle, widths[n], rows_per_pos_cycle), F32)
        return jax.ShapeDtypeStruct((R, widths[n]), dtypes[n])

    names = ["ksb", "vsb", "kd", "vd", "ki", "qsb", "zsb", "qd", "zd", "qi", "kiw", "gsb", "gd",
             "kesb", "vesb", "ked", "kei"]
    widths = dict(f32_widths, qsb=512, zsb=512, qd=512, zd=512, qi=512, kiw=128, gsb=1024, gd=1024,
                  kesb=1024, vesb=1024, ked=512, kei=256)
    dtypes = {n: F32 for n in names}
    for n in ("qsb", "qd", "qi", "kesb", "vesb", "ked", "kei"):
        dtypes[n] = BF16
    vdt_rows = DSA_KV_HEADS * HEAD_DIM
    outs = pl.pallas_call(
        functools.partial(_feat_kernel, feature_major=feature_major),
        grid=(R // tm,),
        in_specs=[
            pl.BlockSpec((tm, D_MODEL), row),
            pl.BlockSpec((1, D_MODEL), const),
            pl.BlockSpec((D_MODEL, _C_END), const, pipeline_mode=pl.Buffered(1)),
            pl.BlockSpec((tm, LANES), pos),
            pl.BlockSpec((tm, LANES), pos),
            pl.BlockSpec((1, LANES), const),
            pl.BlockSpec((1, LANES), const),
            pl.BlockSpec((1, LANES), const),
        ],
        out_specs=[out_spec(n) for n in names]
        + [pl.BlockSpec((tm // key_chunk, vdt_rows, key_chunk), lambda i: (i, 0, 0))],
        out_shape=[out_shape(n) for n in names]
        + [jax.ShapeDtypeStruct((R // key_chunk, vdt_rows, key_chunk), BF16)],
        compiler_params=pltpu.CompilerParams(dimension_semantics=("arbitrary",),
                                             vmem_limit_bytes=VMEM_LIMIT),
        name="features",
    )(x2d, norm_g, w_r, cos, sin, qn, kn, inn)
    return dict(zip(names + ["vdt"], outs))


def _sb_softplus_parts(z):
    e = jnp.exp(-jnp.abs(z))
    return jnp.maximum(z, 0.0) + jnp.log(1.0 + e)


def _sb_kernel(q_ref, ke_ref, ve_ref, o_ref, carry_ref, acc_ref, *, T):
    i = pl.program_id(1)
    rr = lax.broadcasted_iota(I32, (T, T), 0)
    cc = lax.broadcasted_iota(I32, (T, T), 1)
    suffix = (rr > cc).astype(BF16)
    suffix2 = jnp.concatenate([suffix, suffix], axis=0)
    allowed1 = cc < rr
    allowed = jnp.concatenate([allowed1, allowed1], axis=0)

    def widen(v):
        return v if T == LANES else jnp.concatenate([v] * (T // LANES), axis=1)

    def block(j, diag):
        start = pl.multiple_of(j * T, T)
        heads = range(SB_KV_HEADS)
        zs = []
        for kvh in heads:
            q2 = q_ref[:, kvh * LANES:(kvh + 1) * LANES]
            c0 = kvh * 2 * LANES
            ka = ke_ref[pl.ds(start, T), c0:c0 + LANES]
            kb = ke_ref[pl.ds(start, T), c0 + LANES:c0 + 2 * LANES]
            zs.append(jnp.concatenate([_dot_nt(q2, ka), _dot_nt(q2, kb)], axis=0))
        l1ms = [-_sb_softplus_parts(z) for z in zs]
        if diag:
            l1ms = [jnp.where(allowed, l1m, 0.0) for l1m in l1ms]
        tails = [_dot(jnp.concatenate(_split_bf16(l1m), axis=1), suffix2) for l1m in l1ms]
        tots = [jnp.sum(l1m, axis=1, keepdims=True) for l1m in l1ms]
        live = None
        for kvh in heads:
            c0 = kvh * 2 * LANES
            if diag:
                logit = zs[kvh] + l1ms[kvh] + tails[kvh]
                a = jnp.where(allowed, jnp.exp(logit), 0.0).astype(BF16)
                carry = jnp.broadcast_to(tots[kvh], (2 * T, LANES))
            else:
                logit = zs[kvh] + l1ms[kvh] + tails[kvh] + widen(carry_ref[kvh])
                a = jnp.exp(logit).astype(BF16)
                carry = carry_ref[kvh] + tots[kvh]
            va = ve_ref[pl.ds(start, T), c0:c0 + LANES]
            vb = ve_ref[pl.ds(start, T), c0 + LANES:c0 + 2 * LANES]
            o = _dot(a[:T], va) + _dot(a[T:], vb)
            acc_ref[kvh] = o if diag else acc_ref[kvh] + o
            carry_ref[kvh] = carry
            live = carry if live is None else jnp.maximum(live, carry)
        return jnp.max(live)

    def cond(s):
        return jnp.logical_and(s[0] >= 0, s[1] >= SB_DEAD)

    def body(s):
        return s[0] - 1, block(s[0], False)

    lax.while_loop(cond, body, (i - 1, block(i, True)))
    for kvh in range(SB_KV_HEADS):
        o_ref[:, kvh * LANES:(kvh + 1) * LANES] = acc_ref[kvh]


def _sb_attention(qsb, kesb, vesb, B, S, T):
    nq = S // T
    return pl.pallas_call(
        functools.partial(_sb_kernel, T=T),
        grid=(B, nq),
        in_specs=[
            pl.BlockSpec((T, 512), lambda b, i: (b * nq + i, 0)),
            pl.BlockSpec((S, 1024), lambda b, i: (b, 0)),
            pl.BlockSpec((S, 1024), lambda b, i: (b, 0)),
        ],
        out_specs=pl.BlockSpec((T, 512), lambda b, i: (b * nq + i, 0)),
        out_shape=jax.ShapeDtypeStruct((B * S, 512), F32),
        scratch_shapes=[pltpu.VMEM((SB_KV_HEADS, 2 * T, LANES), F32),
                        pltpu.VMEM((SB_KV_HEADS, T, LANES), F32)],
        compiler_params=pltpu.CompilerParams(dimension_semantics=("arbitrary", "arbitrary"),
                                             vmem_limit_bytes=VMEM_LIMIT),
        name="sb_prompt",
    )(qsb, kesb, vesb)


NEG_INF = float("-inf")
COUNT_WIDE = 2


def _canonical_score(score):
    return jnp.where(score == 0.0, 0.0, score)


def _pattern_to_float(pat):
    return pltpu.bitcast(pat ^ ((pat >> 31) & 0x7FFFFFFF), F32)


def _count_rows(score_ref, n_steps, rows, pred):
    width = score_ref.shape[1]

    def body(j, acc):
        start = pl.multiple_of(j * rows, rows)
        scores = score_ref[pl.ds(start, rows), :]
        idx = start + lax.broadcasted_iota(I32, (rows, width), 0)
        hit = jnp.where(pred(scores, idx), 1, 0).astype(I32)
        return acc + hit.reshape(rows // 8, 8, width).sum(axis=0)

    acc = lax.fori_loop(0, n_steps, body, jnp.zeros((8, width), I32))
    return acc.sum(axis=0, keepdims=True)


def _count_rounded(round_ref, n_steps, rows, cand):
    width = round_ref.shape[1]
    one, zero = jnp.ones((), BF16), jnp.zeros((), BF16)

    def body(j, acc):
        start = pl.multiple_of(j * rows, rows)
        hit = jnp.where(round_ref[pl.ds(start, rows), :] >= cand, one, zero)
        parts = [hit[r:r + 16, :] for r in range(0, rows, 16)]
        while len(parts) > 1:
            parts = [a + b for a, b in zip(parts[::2], parts[1::2])]
        return acc + parts[0].astype(F32)

    acc = lax.fori_loop(0, n_steps, body, jnp.zeros((16, width), F32))
    return acc.sum(axis=0, keepdims=True)


def _kth_largest_score(score_ref, round_ref, n_chunks, chunk, topk):
    width = score_ref.shape[1]
    n_steps = (n_chunks + COUNT_WIDE - 1) // COUNT_WIDE
    rows = COUNT_WIDE * chunk

    def coarse_step(k, p16):
        cand = p16 + jnp.left_shift(jnp.int32(1), 15 - k)
        bits16 = (cand ^ ((cand >> 31) & 0x7FFF)) & 0xFFFF
        cand_b = pltpu.bitcast(bits16 << 16, F32).astype(BF16)
        cnt = _count_rounded(round_ref, n_steps, rows, cand_b)
        return jnp.where(cnt >= topk, cand, p16)

    p16 = lax.fori_loop(0, 16, coarse_step, jnp.full((1, width), -32768, I32))
    few = p16 == -32768
    below = jnp.where(few, 0, p16 - 1)
    base = jnp.where(below >= 0, below * 65536, below * 65536 + 65535)

    def fine_step(k, off):
        cand_off = off | jnp.left_shift(jnp.int32(1), 16 - k)
        cand_f = _pattern_to_float(base + cand_off)
        cnt = _count_rows(score_ref, n_steps, rows, lambda s, idx: s >= cand_f)
        return jnp.where(cnt >= topk, cand_off, off)

    off = lax.fori_loop(0, 17, fine_step, jnp.zeros((1, width), I32))
    t = jnp.where(few, NEG_INF, _pattern_to_float(base + off))
    n_gt = _count_rows(score_ref, n_steps, rows, lambda s, idx: s > t)
    return t, jnp.where(few, 0, topk - n_gt)


def _tie_cutoff(score_ref, n_chunks, chunk, t, need, idx_bits):
    width = score_ref.shape[1]
    n_steps = (n_chunks + COUNT_WIDE - 1) // COUNT_WIDE
    rows = COUNT_WIDE * chunk

    def idx_step(k, c):
        cand = c + jnp.left_shift(jnp.int32(1), idx_bits - 1 - k)
        cnt = _count_rows(score_ref, n_steps, rows,
                          lambda s, idx: jnp.logical_and(s == t, idx < cand))
        return jnp.where(cnt < need, cand, c)

    c = lax.fori_loop(0, idx_bits, idx_step, jnp.zeros((1, width), I32))
    return jnp.where(need > 0, c, -1)


def _dsa_kernel(qi_ref, kiw_ref, kei_ref, qd_ref, ked_ref, vdt_ref, o_ref,
                score_ref, round_ref, acc_ref, kmax_ref, *, T, CK, topk):
    i = pl.program_id(1)
    n_full = (i * T) // CK
    n_chunks = n_full + 1
    rr = lax.broadcasted_iota(I32, (CK, T), 0)
    cc = lax.broadcasted_iota(I32, (CK, T), 1)
    w_t = kiw_ref[...].T[IDX_DIM:IDX_DIM + IDX_HEADS, :]

    def score_chunk(j, last):
        start = pl.multiple_of(j * CK, CK)
        ka = kei_ref[pl.ds(start, CK), :LANES]
        kb = kei_ref[pl.ds(start, CK), LANES:]
        prods = []
        for p in range(IDX_HEADS // 2):
            qp = qi_ref[:, p * LANES:(p + 1) * LANES]
            prods += [_dot_nt(ka, qp), _dot_nt(kb, qp)]
        sc = jnp.zeros((CK, T), F32)
        for h, s in enumerate(prods):
            sc = sc + w_t[h:h + 1, :] * jnp.maximum(s, 0.0)
        sc = _canonical_score(sc * IDX_SCALE)
        if last:
            sc = jnp.where(start + rr <= i * T + cc, sc, NEG_INF)
        score_ref[pl.ds(start, CK), :] = sc
        round_ref[pl.ds(start, CK), :] = sc.astype(BF16)

    def score_body(j, carry):
        score_chunk(j, False)
        return carry

    lax.fori_loop(0, n_full, score_body, 0)
    score_chunk(n_full, True)

    @pl.when(n_chunks % COUNT_WIDE != 0)
    def _():
        pad_start = pl.multiple_of(n_chunks * CK, CK)
        score_ref[pl.ds(pad_start, CK), :] = jnp.full((CK, T), NEG_INF, F32)
        round_ref[pl.ds(pad_start, CK), :] = jnp.full((CK, T), NEG_INF, BF16)

    t, need = _kth_largest_score(score_ref, round_ref, n_chunks, CK, topk)
    need_f = need.astype(F32)

    acc_ref[...] = jnp.zeros(acc_ref.shape, F32)
    top_rows = lax.broadcasted_iota(I32, (LANES, T), 0) < HEAD_DIM
    r2 = lax.broadcasted_iota(I32, (CK, CK), 0)
    c2 = lax.broadcasted_iota(I32, (CK, CK), 1)
    prefix = (c2 <= r2).astype(BF16)

    def chunk_inputs(j, tied_before):
        start = pl.multiple_of(j * CK, CK)
        sc = score_ref[pl.ds(start, CK), :]
        tied = sc == t
        rank = tied_before + _dot(prefix, jnp.where(tied, 1.0, 0.0).astype(BF16))
        sel = jnp.logical_or(sc > t, jnp.logical_and(tied, rank <= need_f))
        return start, sel, rank[CK - 1:CK, :]

    def pair_inputs(j, start, p):
        kvh = p // 2
        kcol = kvh * 2 * LANES
        qp = qd_ref[:, p * LANES:(p + 1) * LANES]
        vt = vdt_ref[j, kvh * HEAD_DIM:(kvh + 1) * HEAD_DIM, :]
        kblks = [ked_ref[pl.ds(start, CK), kcol + e * LANES:kcol + (e + 1) * LANES] for e in range(2)]
        return qp, vt, kblks

    def attend_online(j, state):
        ms, ls, tied_before = state
        start, sel, tied_before = chunk_inputs(j, tied_before)
        new_ms, new_ls = [], []
        for p in range(DSA_HEADS // 2):
            qp, vt, kblks = pair_inputs(j, start, p)
            outs, alphas = [], []
            for e in range(2):
                h = 2 * p + e
                lg = jnp.where(sel, _dot_nt(kblks[e], qp), NEG_BIG)
                m_new = jnp.maximum(ms[h], jnp.max(lg, axis=0, keepdims=True))
                alpha = jnp.exp(ms[h] - m_new)
                pr = jnp.where(sel, jnp.exp(lg - m_new), 0.0)
                new_ls.append(alpha * ls[h] + jnp.sum(pr, axis=0, keepdims=True))
                new_ms.append(m_new)
                outs.append(_dot(vt, pr.astype(BF16)))
                alphas.append(alpha)
            scale = jnp.where(top_rows, alphas[0], alphas[1])
            acc_ref[p] = acc_ref[p] * scale + jnp.concatenate(outs, axis=0)
        return tuple(new_ms), tuple(new_ls), tied_before

    def softmax_online(_):
        init = (tuple(jnp.full((1, T), NEG_BIG, F32) for _ in range(DSA_HEADS)),
                tuple(jnp.zeros((1, T), F32) for _ in range(DSA_HEADS)),
                jnp.zeros((1, T), F32))
        return lax.fori_loop(0, n_chunks, attend_online, init)[1]

    def attend_unshifted(j, state):
        ls, tied_before = state
        start, sel, tied_before = chunk_inputs(j, tied_before)
        pairs = [pair_inputs(j, start, p) for p in range(DSA_HEADS // 2)]
        lgs = [_dot_nt(kblks[e], qp) for qp, _, kblks in pairs for e in range(2)]
        prs = [jnp.where(sel, jnp.exp(lg), 0.0) for lg in lgs]
        new_ls = tuple(l + pr.reshape(CK // 8, 8, T).sum(axis=0) for l, pr in zip(ls, prs))
        for p, (_, vt, _) in enumerate(pairs):
            outs = [_dot(vt, prs[2 * p + e].astype(BF16)) for e in range(2)]
            acc_ref[p] = acc_ref[p] + jnp.concatenate(outs, axis=0)
        return new_ls, tied_before

    def softmax_unshifted(_):
        init = (tuple(jnp.zeros((8, T), F32) for _ in range(DSA_HEADS)), jnp.zeros((1, T), F32))
        ls = lax.fori_loop(0, n_chunks, attend_unshifted, init)[0]
        return tuple(l.sum(axis=0, keepdims=True) for l in ls)

    @pl.when(i == 0)
    def _():
        for kvh in range(DSA_KV_HEADS):
            def k_norm(c, m):
                rows = ked_ref[pl.ds(pl.multiple_of(c * CK, CK), CK),
                               kvh * 2 * LANES:kvh * 2 * LANES + LANES].astype(F32)
                return jnp.maximum(m, jnp.max(jnp.sum(rows * rows, axis=1, keepdims=True)))
            kmax_ref[kvh] = lax.fori_loop(0, ked_ref.shape[0] // CK, k_norm, jnp.float32(0.0))

    bound2 = jnp.float32(0.0)
    for p in range(DSA_HEADS // 2):
        qf = qd_ref[:, p * LANES:(p + 1) * LANES].astype(F32)
        q2max = jnp.max(jnp.sum(qf * qf, axis=1, keepdims=True))
        bound2 = jnp.maximum(bound2, q2max * kmax_ref[p // 2])
    ls = lax.cond(bound2 <= LOGIT_SAFE * LOGIT_SAFE, softmax_unshifted, softmax_online, 0)
    for p in range(DSA_HEADS // 2):
        denom = jnp.where(top_rows, ls[2 * p], ls[2 * p + 1])
        o_ref[:, p * LANES:(p + 1) * LANES] = (acc_ref[p] / denom).T


def _dsa_attention(qi, kiw, kei, qd, ked, vdt, B, S, T, topk):
    nq = S // T
    CK = vdt.shape[2]
    assert S % (COUNT_WIDE * CK) == 0 and CK % T == 0
    qrow = lambda b, i: (b * nq + i, 0)
    batch = lambda b, i: (b, 0)
    return pl.pallas_call(
        functools.partial(_dsa_kernel, T=T, CK=CK, topk=topk),
        grid=(B, nq),
        in_specs=[
            pl.BlockSpec((T, 512), qrow),
            pl.BlockSpec((T, LANES), qrow),
            pl.BlockSpec((S, 256), batch),
            pl.BlockSpec((T, 512), qrow),
            pl.BlockSpec((S, 512), batch),
            pl.BlockSpec((S // CK, LANES, CK), lambda b, i: (b, 0, 0)),
        ],
        out_specs=pl.BlockSpec((T, 512), qrow),
        out_shape=jax.ShapeDtypeStruct((B * S, 512), F32),
        scratch_shapes=[
            pltpu.VMEM((S, T), F32),
            pltpu.VMEM((S, T), BF16),
            pltpu.VMEM((DSA_HEADS // 2, LANES, T), F32),
            pltpu.SMEM((DSA_KV_HEADS,), F32),
        ],
        compiler_params=pltpu.CompilerParams(dimension_semantics=("arbitrary", "arbitrary"),
                                             vmem_limit_bytes=VMEM_LIMIT),
        name="dsa_prompt",
    )(qi, kiw, kei, qd, ked, vdt)


def _merge_kernel(x_ref, osb_ref, zsb_ref, od_ref, zd_ref, gsb_ref, gd_ref,
                  wsb_ref, wd_ref, wo_ref, y_ref):
    zsb = zsb_ref[...]
    zd = zd_ref[...]
    a_sb = (osb_ref[...] * (zsb * jax.nn.sigmoid(zsb))).astype(BF16)
    a_d = (od_ref[...] * (zd * jax.nn.sigmoid(zd))).astype(BF16)
    u_sb = _dot(a_sb, wsb_ref[...])
    u_d = _dot(a_d, wd_ref[...])
    mixed = jax.nn.sigmoid(gsb_ref[...]) * u_sb + jax.nn.sigmoid(gd_ref[...]) * u_d
    y_ref[...] = x_ref[...] + _dot(mixed.astype(BF16), wo_ref[...])


def _merge(x2d, osb, zsb, od, zd, gsb, gd, wsb, wd, wo, tm):
    R = x2d.shape[0]
    row = lambda i: (i, 0)
    const = lambda i: (0, 0)
    return pl.pallas_call(
        _merge_kernel,
        grid=(R // tm,),
        in_specs=[
            pl.BlockSpec((tm, D_MODEL), row),
            pl.BlockSpec((tm, 512), row),
            pl.BlockSpec((tm, 512), row),
            pl.BlockSpec((tm, 512), row),
            pl.BlockSpec((tm, 512), row),
            pl.BlockSpec((tm, D_MODEL), row),
            pl.BlockSpec((tm, D_MODEL), row),
            pl.BlockSpec((512, D_MODEL), const),
            pl.BlockSpec((512, D_MODEL), const),
            pl.BlockSpec((D_MODEL, D_MODEL), const),
        ],
        out_specs=pl.BlockSpec((tm, D_MODEL), row),
        out_shape=jax.ShapeDtypeStruct((R, D_MODEL), F32),
        compiler_params=pltpu.CompilerParams(dimension_semantics=("arbitrary",),
                                             vmem_limit_bytes=VMEM_LIMIT),
        name="merge",
    )(x2d, osb, zsb, od, zd, gsb, gd, wsb, wd, wo)


def _rope_tables(pos):
    inv = ROPE_THETA ** (-jnp.arange(HALF, dtype=F32) / HALF)
    ang = pos.astype(F32)[:, None] * inv[None, :]
    cos = jnp.cos(ang)
    sin = jnp.sin(ang)
    return jnp.tile(cos, (1, 4)), jnp.tile(jnp.concatenate([-sin, sin], axis=1), (1, 2))


def _prep_weights(norm_g, w_in, q_norm_g, k_norm_g, idx_k_norm_g, w_up_sb, w_up_dsa, w_out):
    pad = jnp.zeros((D_MODEL, _C_GSB - _C_KI - (_W_SPLIT - _C_KI)), w_in.dtype)
    w_r = jnp.concatenate([w_in[:, :_W_SPLIT], pad, w_in[:, _W_SPLIT:]], axis=1).astype(BF16)
    qn = jnp.tile(q_norm_g, 2)[None, :]
    kn = jnp.tile(k_norm_g, 2)[None, :]
    inn = jnp.concatenate([idx_k_norm_g, jnp.zeros((LANES - IDX_DIM,), F32)])[None, :]
    return (norm_g[None, :], w_r, qn, kn, inn,
            w_up_sb.astype(BF16), w_up_dsa.astype(BF16), w_out.astype(BF16))


def _prompt_layer(x, weights, tiles):
    B, S, _ = x.shape
    t_feat, t_merge, t_sb, t_dsa, key_chunk = tiles
    norm_g, w_r, qn, kn, inn, wsb, wd, wo = weights
    x2d = x.reshape(B * S, D_MODEL)
    cos, sin = _rope_tables(jnp.arange(S))
    f = _features(x2d, S, cos, sin, norm_g, w_r, qn, kn, inn, t_feat, key_chunk, True)
    topk = min(DSA_TOPK_MAX, S // 4)
    osb = _sb_attention(f["qsb"], f["kesb"], f["vesb"], B, S, t_sb)
    od = _dsa_attention(f["qi"], f["kiw"], f["kei"], f["qd"], f["ked"], f["vdt"], B, S, t_dsa, topk)
    y = _merge(x2d, osb, f["zsb"], od, f["zd"], f["gsb"], f["gd"], wsb, wd, wo, t_merge)

    def rows(a, heads):
        return jnp.moveaxis(a.reshape(B, heads, HEAD_DIM, S), 3, 1)[None]

    return (y.reshape(B, S, D_MODEL),
            rows(f["ksb"], SB_KV_HEADS), rows(f["vsb"], SB_KV_HEADS),
            rows(f["kd"], DSA_KV_HEADS), rows(f["vd"], DSA_KV_HEADS),
            jnp.moveaxis(f["ki"], 2, 1)[None])


PAGE_UNROLL = 32


def _page_copy(cache_hbm, pt_ref, buf, sems, ci, seq, slot, p):
    return pltpu.make_async_copy(cache_hbm.at[pt_ref[seq, p]], buf.at[slot, p], sems.at[ci, slot])


def _stream_pages(caches, pt_ref, bufs, sems, n_pages):
    b = pl.program_id(0)
    nb = pl.num_programs(0)
    slot = b % 2

    def start_all_pages(seq, slot_):
        def body(p, carry):
            for ci, (cache, buf) in enumerate(zip(caches, bufs)):
                _page_copy(cache, pt_ref, buf, sems, ci, seq, slot_, p).start()
            return carry
        lax.fori_loop(0, n_pages, body, 0, unroll=min(8, n_pages))

    @pl.when(b == 0)
    def _():
        start_all_pages(0, 0)

    @pl.when(b + 1 < nb)
    def _():
        start_all_pages(b + 1, 1 - slot)

    for p in range(n_pages):
        for ci, (cache, buf) in enumerate(zip(caches, bufs)):
            _page_copy(cache, pt_ref, buf, sems, ci, b, slot, p).wait()
    return slot


def _dec_sb_kernel(pt_ref, qm_ref, k_hbm, v_hbm, o_ref, kfirst, vfirst, kmore, vmore, sems, *, n_pages, G):
    b = pl.program_id(0)
    nb = pl.num_programs(0)
    slot = b % 2
    n_groups = n_pages // G
    qm = qm_ref[0]
    P = PAGE_SIZE
    rr = lax.broadcasted_iota(I32, (P, P), 0)
    cc = lax.broadcasted_iota(I32, (P, P), 1)
    suffix = (rr > cc).astype(BF16)
    ones = jnp.ones((P, LANES), BF16)

    def group_copies(seq, g, kdst, vdst, ksem, vsem):
        copies = []
        for r in range(G):
            page = pt_ref[seq, n_pages - (g + 1) * G + r]
            copies.append(pltpu.make_async_copy(k_hbm.at[page], kdst.at[r], ksem))
            copies.append(pltpu.make_async_copy(v_hbm.at[page], vdst.at[r], vsem))
        return copies

    def first_copies(seq, s):
        return group_copies(seq, 0, kfirst.at[s], vfirst.at[s], sems.at[0, s], sems.at[1, s])

    def process(kb, vb, carry, acc):
        zs, l1ms, tails, tots = [], [], [], []
        for r in range(G):
            z = _dot(qm, kb[r].astype(BF16))
            l1m = -_sb_softplus_parts(z)
            hi, lo = _split_bf16(l1m)
            zs.append(z)
            l1ms.append(l1m)
            tails.append(_dot(hi, suffix) + _dot(lo, suffix))
            tots.append(_dot(hi, ones) + _dot(lo, ones))
        for r in reversed(range(G)):
            a = jnp.exp(zs[r] + l1ms[r] + tails[r] + carry).astype(BF16)
            acc = acc + _dot_nt(a, vb[r].astype(BF16))
            carry = carry + tots[r]
        return carry, acc

    @pl.when(b == 0)
    def _():
        for c in first_copies(0, 0):
            c.start()

    @pl.when(b + 1 < nb)
    def _():
        for c in first_copies(b + 1, 1 - slot):
            c.start()

    for c in first_copies(b, slot):
        c.wait()
    n_heads = qm.shape[0]
    carry, acc = process(kfirst.at[slot], vfirst.at[slot],
                         jnp.zeros((n_heads, LANES), F32), jnp.zeros((n_heads, qm.shape[1]), F32))

    def cond(s):
        return jnp.logical_and(s[0] < n_groups, s[3] >= SB_DEAD)

    def body(s):
        g, carry, acc, _ = s
        copies = group_copies(b, g, kmore, vmore, sems.at[2, 0], sems.at[2, 1])
        for c in copies:
            c.start()
        for c in copies:
            c.wait()
        carry, acc = process(kmore, vmore, carry, acc)
        return g + 1, carry, acc, jnp.max(carry)

    _, _, acc, _ = lax.while_loop(cond, body, (jnp.int32(1), carry, acc, jnp.max(carry)))
    o_ref[0] = acc


def _dec_sb(page_table, qm, cache_k, cache_v, group):
    DB, n_pages = page_table.shape
    width = cache_k.shape[1]
    return pl.pallas_call(
        functools.partial(_dec_sb_kernel, n_pages=n_pages, G=group),
        grid_spec=pltpu.PrefetchScalarGridSpec(
            num_scalar_prefetch=1,
            grid=(DB,),
            in_specs=[
                pl.BlockSpec((1, SB_HEADS, width), lambda b, pt: (b, 0, 0)),
                pl.BlockSpec(memory_space=pl.ANY),
                pl.BlockSpec(memory_space=pl.ANY),
            ],
            out_specs=pl.BlockSpec((1, SB_HEADS, width), lambda b, pt: (b, 0, 0)),
            scratch_shapes=[
                pltpu.VMEM((2, group, width, PAGE_SIZE), F32),
                pltpu.VMEM((2, group, width, PAGE_SIZE), F32),
                pltpu.VMEM((group, width, PAGE_SIZE), F32),
                pltpu.VMEM((group, width, PAGE_SIZE), F32),
                pltpu.SemaphoreType.DMA((3, 2)),
            ]),
        out_shape=jax.ShapeDtypeStruct((DB, SB_HEADS, width), F32),
        compiler_params=pltpu.CompilerParams(dimension_semantics=("arbitrary",),
                                             vmem_limit_bytes=VMEM_LIMIT),
        name="dec_sb",
    )(page_table, qm, cache_k, cache_v)


def _dec_idx_kernel(pt_ref, qi_ref, w_ref, kinew_ref, k_hbm, score_ref, kbuf, sems, *, n_pages, n_rows):
    slot = _stream_pages((k_hbm,), pt_ref, (kbuf,), sems, n_pages)
    qi = qi_ref[0]
    w = w_ref[0]

    def score(pp, carry):
        kk = jnp.concatenate([kbuf[slot, 2 * pp], kbuf[slot, 2 * pp + 1]], axis=1).astype(BF16)
        s = _dot(qi, kk)
        sc = _canonical_score(jnp.sum(w * jnp.maximum(s, 0.0), axis=0, keepdims=True) * IDX_SCALE)
        score_ref[0, pl.ds(2 * pp, 1), :] = sc[:, :PAGE_SIZE]
        score_ref[0, pl.ds(2 * pp + 1, 1), :] = sc[:, PAGE_SIZE:]
        return carry

    lax.fori_loop(0, n_pages // 2, score, 0, unroll=min(PAGE_UNROLL // 2, n_pages // 2))
    s_new = jnp.sum(qi.astype(F32) * kinew_ref[0].astype(F32), axis=1, keepdims=True)
    sc_new = jnp.sum(w * jnp.maximum(s_new, 0.0), axis=0, keepdims=True) * IDX_SCALE
    tail_rows = n_rows - n_pages
    r = lax.broadcasted_iota(I32, (tail_rows, LANES), 0)
    c = lax.broadcasted_iota(I32, (tail_rows, LANES), 1)
    new_row = jnp.broadcast_to(_canonical_score(sc_new), (tail_rows, LANES))
    score_ref[0, n_pages:, :] = jnp.where(jnp.logical_and(r == 0, c == 0), new_row, NEG_INF)


def _dec_idx(page_table, qi, w, kinew, cache_k, n_rows):
    DB, n_pages = page_table.shape
    per_seq = lambda b, pt: (b, 0, 0)
    return pl.pallas_call(
        functools.partial(_dec_idx_kernel, n_pages=n_pages, n_rows=n_rows),
        grid_spec=pltpu.PrefetchScalarGridSpec(
            num_scalar_prefetch=1,
            grid=(DB,),
            in_specs=[
                pl.BlockSpec((1, IDX_HEADS, IDX_DIM), per_seq),
                pl.BlockSpec((1, IDX_HEADS, 1), per_seq),
                pl.BlockSpec((1, 1, IDX_DIM), per_seq),
                pl.BlockSpec(memory_space=pl.ANY),
            ],
            out_specs=pl.BlockSpec((1, n_rows, LANES), per_seq),
            scratch_shapes=[
                pltpu.VMEM((2, n_pages, IDX_DIM, PAGE_SIZE), F32),
                pltpu.SemaphoreType.DMA((1, 2)),
            ]),
        out_shape=jax.ShapeDtypeStruct((DB, n_rows, LANES), F32),
        compiler_params=pltpu.CompilerParams(dimension_semantics=("arbitrary",),
                                             vmem_limit_bytes=VMEM_LIMIT),
        name="dec_idx",
    )(page_table, qi, w, kinew, cache_k)


def _dec_thr_kernel(scores_ref, t_ref, c_ref, score_scr, round_scr, *, n_rows, topk, idx_bits):
    for p in range(n_rows):
        blk = scores_ref[:, p * LANES:(p + 1) * LANES].T
        score_scr[p * LANES:(p + 1) * LANES, :] = blk
        round_scr[p * LANES:(p + 1) * LANES, :] = blk.astype(BF16)
    n_chunks, chunk = n_rows // 2, 2 * LANES
    t, need = _kth_largest_score(score_scr, round_scr, n_chunks, chunk, topk)
    t_ref[...] = t
    c_ref[...] = _tie_cutoff(score_scr, n_chunks, chunk, t, need, idx_bits)


def _dec_thr(scores2d, n_rows, topk):
    DB = scores2d.shape[0]
    assert n_rows % (2 * COUNT_WIDE) == 0
    idx_bits = (n_rows * LANES - 1).bit_length()
    return pl.pallas_call(
        functools.partial(_dec_thr_kernel, n_rows=n_rows, topk=topk, idx_bits=idx_bits),
        out_shape=[jax.ShapeDtypeStruct((1, DB), F32), jax.ShapeDtypeStruct((1, DB), I32)],
        scratch_shapes=[pltpu.VMEM((n_rows * LANES, DB), F32), pltpu.VMEM((n_rows * LANES, DB), BF16)],
        compiler_params=pltpu.CompilerParams(vmem_limit_bytes=VMEM_LIMIT),
        name="dec_thr",
    )(scores2d)


def _dec_dsa_kernel(pt_ref, c_ref, t_ref, qm_ref, scores_ref, kdnew_ref, vdnew_ref, k_hbm, v_hbm, o_ref,
                    kbuf, vbuf, lbuf, sems, *, n_pages):
    b = pl.program_id(0)
    slot = _stream_pages((k_hbm, v_hbm), pt_ref, (kbuf, vbuf), sems, n_pages)
    qm = qm_ref[0]
    t = t_ref[0]
    cidx = c_ref[b]
    lane = lax.broadcasted_iota(I32, (1, LANES), 1)

    def selected(p):
        sc = scores_ref[0, pl.ds(p, 1), :]
        idx = p * PAGE_SIZE + lane
        return jnp.logical_or(sc > t, jnp.logical_and(sc == t, idx <= cidx))

    def page_pair(buf, pp):
        return jnp.concatenate([buf[slot, 2 * pp], buf[slot, 2 * pp + 1]], axis=1).astype(BF16)

    def logits(pp, m):
        sel = jnp.concatenate([selected(2 * pp), selected(2 * pp + 1)], axis=1)
        lg = jnp.where(sel, _dot(qm, page_pair(kbuf, pp)), NEG_BIG)
        lbuf[pp] = lg
        return jnp.maximum(m, lg)

    n_heads = qm.shape[0]
    n_pairs = n_pages // 2
    unroll = min(PAGE_UNROLL // 2, n_pairs)
    m = lax.fori_loop(0, n_pairs, logits, jnp.full((n_heads, 2 * LANES), NEG_BIG, F32), unroll=unroll)
    sel_new = selected(n_pages)[:, :1]
    lg_new = jnp.sum(qm.astype(F32) * kdnew_ref[0].astype(F32), axis=1, keepdims=True)
    lg_new = jnp.where(sel_new, lg_new, NEG_BIG)
    m = jnp.maximum(jnp.max(m, axis=1, keepdims=True), lg_new)

    def attend(pp, state):
        l, acc = state
        pr = jnp.exp(lbuf[pp] - m)
        return l + pr, acc + _dot_nt(pr.astype(BF16), page_pair(vbuf, pp))

    l, acc = lax.fori_loop(0, n_pairs, attend,
                           (jnp.zeros((n_heads, 2 * LANES), F32), jnp.zeros((n_heads, LANES), F32)),
                           unroll=unroll)
    pr_new = jnp.where(sel_new, jnp.exp(lg_new - m), 0.0)
    pr_new_b = pr_new.astype(BF16).astype(F32)
    acc = acc + pr_new_b * vdnew_ref[0].astype(F32)
    denom = jnp.sum(l, axis=1, keepdims=True) + pr_new
    o_ref[0] = acc / denom


def _dec_dsa(page_table, cidx, t_rep, qm, scores, kdnew, vdnew, cache_k, cache_v):
    DB, n_pages = page_table.shape
    n_rows = scores.shape[1]
    per_seq = lambda b, pt, c: (b, 0, 0)
    return pl.pallas_call(
        functools.partial(_dec_dsa_kernel, n_pages=n_pages),
        grid_spec=pltpu.PrefetchScalarGridSpec(
            num_scalar_prefetch=2,
            grid=(DB,),
            in_specs=[
                pl.BlockSpec((1, 1, LANES), per_seq),
                pl.BlockSpec((1, DSA_HEADS, LANES), per_seq),
                pl.BlockSpec((1, n_rows, LANES), per_seq),
                pl.BlockSpec((1, 1, LANES), per_seq),
                pl.BlockSpec((1, 1, LANES), per_seq),
                pl.BlockSpec(memory_space=pl.ANY),
                pl.BlockSpec(memory_space=pl.ANY),
            ],
            out_specs=pl.BlockSpec((1, DSA_HEADS, LANES), per_seq),
            scratch_shapes=[
                pltpu.VMEM((2, n_pages, PAGE_SIZE, LANES), F32),
                pltpu.VMEM((2, n_pages, PAGE_SIZE, LANES), F32),
                pltpu.VMEM((n_pages // 2, DSA_HEADS, 2 * LANES), F32),
                pltpu.SemaphoreType.DMA((2, 2)),
            ]),
        out_shape=jax.ShapeDtypeStruct((DB, DSA_HEADS, LANES), F32),
        compiler_params=pltpu.CompilerParams(dimension_semantics=("arbitrary",),
                                             vmem_limit_bytes=VMEM_LIMIT),
        name="dec_dsa",
    )(page_table, cidx, t_rep, qm, scores, kdnew, vdnew, cache_k, cache_v)


def _heads_in_kv_lanes(q, n_heads, n_kv):
    R = q.shape[0]
    q3 = q.reshape(R, n_heads, 1, HEAD_DIM)
    kv_of_head = jnp.arange(n_heads) // (n_heads // n_kv)
    onehot = (kv_of_head[:, None] == jnp.arange(n_kv)[None, :])[None, :, :, None]
    return jnp.where(onehot, q3, jnp.zeros((), q.dtype)).reshape(R, n_heads, n_kv * HEAD_DIM)


def _own_kv_lanes(o, n_heads, n_kv):
    R = o.shape[0]
    o4 = o.reshape(R, n_heads, n_kv, HEAD_DIM)
    group = n_heads // n_kv
    return jnp.concatenate([o4[:, h, h // group, :] for h in range(n_heads)], axis=1)


def _sample_layer(x, caches, page_table, weights):
    DB, T, _ = x.shape
    c_sb_k, c_sb_v, c_dsa_k, c_dsa_v, c_idx_k = caches
    norm_g, w_r, qn, kn, inn, wsb, wd, wo = weights
    n_pages = page_table.shape[1]
    past = n_pages * PAGE_SIZE
    x2d = x.reshape(DB * T, D_MODEL)
    cos, sin = _rope_tables(jnp.full((DB * T,), past))
    f = _features(x2d, DB * T, cos, sin, norm_g, w_r, qn, kn, inn, DB * T, DB * T, False)

    n_pool = c_sb_k.shape[0]

    def pages_t(c):
        return jnp.moveaxis(c, 1, -1).reshape(n_pool, -1, PAGE_SIZE)

    qm_sb = _heads_in_kv_lanes(f["qsb"], SB_HEADS, SB_KV_HEADS)
    o_sb = _dec_sb(page_table, qm_sb, pages_t(c_sb_k), pages_t(c_sb_v), 2)
    osb = _own_kv_lanes(o_sb, SB_HEADS, SB_KV_HEADS)

    n_rows = n_pages + 8
    qi3 = f["qi"].reshape(DB, IDX_HEADS, IDX_DIM)
    w3 = f["kiw"][:, IDX_DIM:IDX_DIM + IDX_HEADS].reshape(DB, IDX_HEADS, 1)
    kinew = f["kei"][:, :IDX_DIM].reshape(DB, 1, IDX_DIM)
    scores = _dec_idx(page_table, qi3, w3, kinew, pages_t(c_idx_k), n_rows)
    topk = min(DSA_TOPK_MAX, (past + T) // 4)
    t, cidx = _dec_thr(scores.reshape(DB, n_rows * LANES), n_rows, topk)
    t_rep = jnp.broadcast_to(t.reshape(DB, 1, 1), (DB, 1, LANES))

    qm_d = _heads_in_kv_lanes(f["qd"], DSA_HEADS, DSA_KV_HEADS)
    kdnew = f["kd"].astype(BF16).reshape(DB, 1, LANES)
    vdnew = f["vd"].astype(BF16).reshape(DB, 1, LANES)
    o_d = _dec_dsa(page_table, cidx.reshape(DB), t_rep, qm_d, scores, kdnew, vdnew,
                   pages_t(c_dsa_k), pages_t(c_dsa_v))
    od = _own_kv_lanes(o_d, DSA_HEADS, DSA_KV_HEADS)

    y = _merge(x2d, osb, f["zsb"], od, f["zd"], f["gsb"], f["gd"], wsb, wd, wo, DB * T)
    return (y.reshape(DB, T, D_MODEL),
            f["ksb"].reshape(1, DB, T, SB_KV_HEADS, HEAD_DIM),
            f["vsb"].reshape(1, DB, T, SB_KV_HEADS, HEAD_DIM),
            f["kd"].reshape(1, DB, T, DSA_KV_HEADS, HEAD_DIM),
            f["vd"].reshape(1, DB, T, DSA_KV_HEADS, HEAD_DIM),
            f["ki"].reshape(1, DB, T, IDX_DIM))


def kernel(x_prompt, x_sample, cache_sb_k, cache_sb_v, cache_dsa_k, cache_dsa_v, cache_idx_k, page_table,
           norm_g, w_in, q_norm_g, k_norm_g, idx_k_norm_g, w_up_sb, w_up_dsa, w_out):
    assert norm_g.shape[0] == 1 and x_sample.shape[1] == 1
    weights = _prep_weights(norm_g[0], w_in[0], q_norm_g[0], k_norm_g[0], idx_k_norm_g[0],
                            w_up_sb[0], w_up_dsa[0], w_out[0])
    p = _prompt_layer(x_prompt, weights, PROMPT_TILES)
    caches = (cache_sb_k[0], cache_sb_v[0], cache_dsa_k[0], cache_dsa_v[0], cache_idx_k[0])
    s = _sample_layer(x_sample, caches, page_table, weights)
    return (p[0], s[0], p[1], p[2], p[3], p[4], p[5], s[1], s[2], s[3], s[4], s[5])
```

```python
import functools

import jax
import jax.numpy as jnp
from jax import lax
from jax.experimental import pallas as pl
from jax.experimental.pallas import tpu as pltpu

F32 = jnp.float32
BF16 = jnp.bfloat16
I32 = jnp.int32
I16 = jnp.int16

LANES = 128
HEAD_DIM = 64
HALF = HEAD_DIM // 2
SB_HEADS, SB_KV_HEADS = 8, 4
DSA_HEADS, DSA_KV_HEADS = 8, 2
IDX_HEADS, IDX_DIM = 8, 64
D_MODEL = 1024
PAGE_SIZE = 128
DSA_TOPK_MAX = 256
ROPE_THETA = 10000.0
NORM_EPS = 1e-6
ATTN_SCALE = HEAD_DIM ** -0.5
IDX_SCALE = (IDX_HEADS * IDX_DIM) ** -0.5
INT_MIN = -2 ** 31
NEG_BIG = -1e30
SB_DEAD = -110.0
LOGIT_SAFE = 40.0
VMEM_LIMIT = 56 * 1024 * 1024
PROMPT_TILES = (256, 256, 128, 256, 256)

_C_QSB, _C_KSB, _C_VSB, _C_ZSB = 0, 512, 768, 1024
_C_QD, _C_KD, _C_VD, _C_ZD = 1536, 2048, 2176, 2304
_C_QI, _C_KI, _C_GSB, _C_GD, _C_END = 2816, 3328, 3456, 4480, 5504
_W_SPLIT = 3400


def _dot(a, b):
    return jnp.dot(a, b, preferred_element_type=F32)


def _dot_nt(a, b):
    return lax.dot_general(a, b, (((1,), (1,)), ((), ())), preferred_element_type=F32)


def _dot_tn(a, b):
    return lax.dot_general(a, b, (((0,), (0,)), ((), ())), preferred_element_type=F32)


def _split_bf16(x):
    hi = x.astype(BF16)
    lo = (x - hi.astype(F32)).astype(BF16)
    return hi, lo


def _feat_kernel(x_ref, g_ref, wa_ref, wb_ref, cos_ref, sin_ref, qn_ref, kn_ref, in_ref,
                 ksb_ref, vsb_ref, kd_ref, vd_ref, ki_ref,
                 qsb_ref, zsb_ref, qd_ref, zd_ref, qi_ref, kiw_ref, gsb_ref, gd_ref,
                 kesb_ref, vesb_ref, ked_ref, kei_ref, vdt_ref, *, feature_major):
    def put_rows(ref, val):
        if feature_major:
            ref[0] = val.T
        else:
            ref[...] = val

    x = x_ref[...]
    ms = jnp.mean(x * x, axis=-1, keepdims=True)
    xn = (x * lax.rsqrt(ms + NORM_EPS) * g_ref[...]).astype(BF16)

    tm = x.shape[0]
    lane = lax.broadcasted_iota(I32, (tm, LANES), 1)
    lo_half = lane < HEAD_DIM
    lo_rot = (lane % HEAD_DIM) < HALF
    r = lax.broadcasted_iota(I32, (LANES, LANES), 0) // HEAD_DIM
    c = lax.broadcasted_iota(I32, (LANES, LANES), 1) // HEAD_DIM
    group_ones = (r == c).astype(BF16)
    group_ones2 = jnp.concatenate([group_ones, group_ones], axis=0)
    cos = cos_ref[...]
    sin = sin_ref[...]

    def proj(c0, c1):
        if c1 <= _C_GSB:
            return _dot(xn, wa_ref[:, c0:c1])
        return _dot(xn, wb_ref[:, c0 - _C_GSB:c1 - _C_GSB])

    def head_norm(v, gain):
        ss = _dot(jnp.concatenate(_split_bf16(v * v), axis=1), group_ones2)
        return v * lax.rsqrt(ss * (1.0 / HEAD_DIM) + NORM_EPS) * gain

    def rope(v):
        rot = jnp.where(lo_rot, pltpu.roll(v, LANES - HALF, 1), pltpu.roll(v, HALF, 1))
        return v * cos + rot * sin

    def expand(pair, e):
        if e == 0:
            a = jnp.where(lo_half, pair, 0.0)
            return a, pltpu.roll(a, HEAD_DIM, 1)
        b = jnp.where(lo_half, 0.0, pair)
        return pltpu.roll(b, HEAD_DIM, 1), b

    def store_expanded(dst_ref, val, n_heads):
        for h in range(n_heads):
            pair = val[:, (h // 2) * LANES:(h // 2 + 1) * LANES]
            a, b = expand(pair, h % 2)
            dst_ref[:, h * 2 * LANES:h * 2 * LANES + LANES] = a.astype(BF16)
            dst_ref[:, h * 2 * LANES + LANES:(h + 1) * 2 * LANES] = b.astype(BF16)

    qsb_ref[...] = (proj(_C_QSB, _C_KSB) * ATTN_SCALE).astype(BF16)
    ksb = proj(_C_KSB, _C_VSB)
    put_rows(ksb_ref, ksb)
    store_expanded(kesb_ref, ksb, SB_KV_HEADS)
    vsb = proj(_C_VSB, _C_ZSB)
    put_rows(vsb_ref, vsb)
    store_expanded(vesb_ref, vsb, SB_KV_HEADS)
    zsb_ref[...] = proj(_C_ZSB, _C_QD).astype(BF16)

    qd = proj(_C_QD, _C_KD)
    for p in range(DSA_HEADS // 2):
        blk = rope(head_norm(qd[:, p * LANES:(p + 1) * LANES], qn_ref[...]))
        qd_ref[:, p * LANES:(p + 1) * LANES] = (blk * ATTN_SCALE).astype(BF16)
    kd = rope(head_norm(proj(_C_KD, _C_VD), kn_ref[...]))
    put_rows(kd_ref, kd)
    store_expanded(ked_ref, kd, DSA_KV_HEADS)
    vd = proj(_C_VD, _C_ZD)
    put_rows(vd_ref, vd)
    ck = vdt_ref.shape[2]
    for c in range(vdt_ref.shape[0]):
        vdt_ref[c] = vd[c * ck:(c + 1) * ck, :].T.astype(BF16)
    zd_ref[...] = proj(_C_ZD, _C_QI).astype(BF16)

    qi = proj(_C_QI, _C_KI)
    for p in range(IDX_HEADS // 2):
        qi_ref[:, p * LANES:(p + 1) * LANES] = rope(qi[:, p * LANES:(p + 1) * LANES]).astype(BF16)
    kiw = proj(_C_KI, _C_GSB)
    kiw_ref[...] = kiw
    ki = rope(head_norm(kiw, in_ref[...]))
    if feature_major:
        ki_ref[0] = ki.T[:IDX_DIM, :]
    else:
        ki_ref[...] = ki[:, :IDX_DIM]
    kei_ref[:, :LANES] = ki.astype(BF16)
    kei_ref[:, LANES:] = pltpu.roll(ki, HEAD_DIM, 1).astype(BF16)

    gsb_ref[...] = proj(_C_GSB, _C_GD).astype(BF16)
    gd_ref[...] = proj(_C_GD, _C_END).astype(BF16)


def _features(x2d, rows_per_pos_cycle, cos, sin, norm_g, w_r, qn, kn, inn, tm, key_chunk, feature_major):
    R = x2d.shape[0]
    assert tm % key_chunk == 0
    n_pos_blocks = rows_per_pos_cycle // tm
    row = lambda i: (i, 0)
    const = lambda i: (0, 0)
    pos = lambda i: (i % n_pos_blocks, 0)
    f32_widths = dict(ksb=256, vsb=256, kd=128, vd=128, ki=64)

    def out_spec(n):
        if feature_major and n in f32_widths:
            return pl.BlockSpec((1, widths[n], tm), lambda i: (i // n_pos_blocks, 0, i % n_pos_blocks))
        return pl.BlockSpec((tm, widths[n]), row)

    def out_shape(n):
        if feature_major and n in f32_widths:
            return jax.ShapeDtypeStruct((R // rows_per_pos_cycle, widths[n], rows_per_pos_cycle), F32)
        return jax.ShapeDtypeStruct((R, widths[n]), dtypes[n])

    names = ["ksb", "vsb", "kd", "vd", "ki", "qsb", "zsb", "qd", "zd", "qi", "kiw", "gsb", "gd",
             "kesb", "vesb", "ked", "kei"]
    widths = dict(f32_widths, qsb=512, zsb=512, qd=512, zd=512, qi=512, kiw=128, gsb=1024, gd=1024,
                  kesb=1024, vesb=1024, ked=512, kei=256)
    dtypes = {n: F32 for n in names}
    for n in ("qsb", "qd", "qi", "kesb", "vesb", "ked", "kei", "zsb", "zd", "gsb", "gd"):
        dtypes[n] = BF16
    vdt_rows = DSA_KV_HEADS * HEAD_DIM
    outs = pl.pallas_call(
        functools.partial(_feat_kernel, feature_major=feature_major),
        grid=(R // tm,),
        in_specs=[
            pl.BlockSpec((tm, D_MODEL), row),
            pl.BlockSpec((1, D_MODEL), const),
            pl.BlockSpec((D_MODEL, _C_GSB), const, pipeline_mode=pl.Buffered(1)),
            pl.BlockSpec((D_MODEL, _C_END - _C_GSB), const, pipeline_mode=pl.Buffered(1)),
            pl.BlockSpec((tm, LANES), pos),
            pl.BlockSpec((tm, LANES), pos),
            pl.BlockSpec((1, LANES), const),
            pl.BlockSpec((1, LANES), const),
            pl.BlockSpec((1, LANES), const),
        ],
        out_specs=[out_spec(n) for n in names]
        + [pl.BlockSpec((tm // key_chunk, vdt_rows, key_chunk), lambda i: (i, 0, 0))],
        out_shape=[out_shape(n) for n in names]
        + [jax.ShapeDtypeStruct((R // key_chunk, vdt_rows, key_chunk), BF16)],
        compiler_params=pltpu.CompilerParams(dimension_semantics=("arbitrary",),
                                             vmem_limit_bytes=VMEM_LIMIT),
        name="features",
    )(x2d, norm_g, *w_r, cos, sin, qn, kn, inn)
    return dict(zip(names + ["vdt"], outs))


def _sb_softplus_parts(z):
    e = jnp.exp(-jnp.abs(z))
    return jnp.maximum(z, 0.0) + jnp.log(1.0 + e)


def _sb_kernel(q_ref, ke_ref, ve_ref, o_ref, carry_ref, acc_ref, *, T):
    i = pl.program_id(1)
    rr = lax.broadcasted_iota(I32, (T, T), 0)
    cc = lax.broadcasted_iota(I32, (T, T), 1)
    suffix = (rr > cc).astype(BF16)
    suffix2 = jnp.concatenate([suffix, suffix], axis=0)
    allowed1 = cc < rr
    allowed = jnp.concatenate([allowed1, allowed1], axis=0)

    def widen(v):
        return v if T == LANES else jnp.concatenate([v] * (T // LANES), axis=1)

    def block(j, diag):
        start = pl.multiple_of(j * T, T)
        heads = range(SB_KV_HEADS)
        zs = []
        for kvh in heads:
            q2 = q_ref[:, kvh * LANES:(kvh + 1) * LANES]
            c0 = kvh * 2 * LANES
            ka = ke_ref[pl.ds(start, T), c0:c0 + LANES]
            kb = ke_ref[pl.ds(start, T), c0 + LANES:c0 + 2 * LANES]
            zs.append(jnp.concatenate([_dot_nt(q2, ka), _dot_nt(q2, kb)], axis=0))
        l1ms = [-_sb_softplus_parts(z) for z in zs]
        if diag:
            l1ms = [jnp.where(allowed, l1m, 0.0) for l1m in l1ms]
        tails = [_dot(jnp.concatenate(_split_bf16(l1m), axis=1), suffix2) for l1m in l1ms]
        tots = [jnp.sum(l1m, axis=1, keepdims=True) for l1m in l1ms]
        live = None
        for kvh in heads:
            c0 = kvh * 2 * LANES
            if diag:
                logit = zs[kvh] + l1ms[kvh] + tails[kvh]
                a = jnp.where(allowed, jnp.exp(logit), 0.0).astype(BF16)
                carry = jnp.broadcast_to(tots[kvh], (2 * T, LANES))
            else:
                logit = zs[kvh] + l1ms[kvh] + tails[kvh] + widen(carry_ref[kvh])
                a = jnp.exp(logit).astype(BF16)
                carry = carry_ref[kvh] + tots[kvh]
            va = ve_ref[pl.ds(start, T), c0:c0 + LANES]
            vb = ve_ref[pl.ds(start, T), c0 + LANES:c0 + 2 * LANES]
            o = _dot(a[:T], va) + _dot(a[T:], vb)
            acc_ref[kvh] = o if diag else acc_ref[kvh] + o
            carry_ref[kvh] = carry
            live = carry if live is None else jnp.maximum(live, carry)
        return jnp.max(live)

    def cond(s):
        return jnp.logical_and(s[0] >= 0, s[1] >= SB_DEAD)

    def body(s):
        return s[0] - 1, block(s[0], False)

    lax.while_loop(cond, body, (i - 1, block(i, True)))
    for kvh in range(SB_KV_HEADS):
        o_ref[:, kvh * LANES:(kvh + 1) * LANES] = acc_ref[kvh].astype(o_ref.dtype)


def _sb_attention(qsb, kesb, vesb, B, S, T):
    nq = S // T
    return pl.pallas_call(
        functools.partial(_sb_kernel, T=T),
        grid=(B, nq),
        in_specs=[
            pl.BlockSpec((T, 512), lambda b, i: (b * nq + i, 0)),
            pl.BlockSpec((S, 1024), lambda b, i: (b, 0)),
            pl.BlockSpec((S, 1024), lambda b, i: (b, 0)),
        ],
        out_specs=pl.BlockSpec((T, 512), lambda b, i: (b * nq + i, 0)),
        out_shape=jax.ShapeDtypeStruct((B * S, 512), BF16),
        scratch_shapes=[pltpu.VMEM((SB_KV_HEADS, 2 * T, LANES), F32),
                        pltpu.VMEM((SB_KV_HEADS, T, LANES), F32)],
        compiler_params=pltpu.CompilerParams(dimension_semantics=("arbitrary", "arbitrary"),
                                             vmem_limit_bytes=VMEM_LIMIT),
        name="sb_prompt",
    )(qsb, kesb, vesb)


NEG_INF = float("-inf")
COUNT_WIDE = 2


def _canonical_score(score):
    return jnp.where(score == 0.0, 0.0, score)


def _pattern_to_float(pat):
    return pltpu.bitcast(pat ^ ((pat >> 31) & 0x7FFFFFFF), F32)


def _count_rows(score_ref, n_steps, rows, pred):
    width = score_ref.shape[1]

    def body(j, acc):
        start = pl.multiple_of(j * rows, rows)
        scores = score_ref[pl.ds(start, rows), :]
        idx = start + lax.broadcasted_iota(I32, (rows, width), 0)
        hit = jnp.where(pred(scores, idx), 1, 0).astype(I32)
        return acc + hit.reshape(rows // 8, 8, width).sum(axis=0)

    acc = lax.fori_loop(0, n_steps, body, jnp.zeros((8, width), I32))
    return acc.sum(axis=0, keepdims=True)


def _count_rounded(round_ref, n_steps, rows, cand):
    width = round_ref.shape[1]
    one, zero = jnp.ones((), BF16), jnp.zeros((), BF16)

    def body(j, acc):
        start = pl.multiple_of(j * rows, rows)
        hit = jnp.where(round_ref[pl.ds(start, rows), :] >= cand, one, zero)
        parts = [hit[r:r + 16, :] for r in range(0, rows, 16)]
        while len(parts) > 1:
            parts = [a + b for a, b in zip(parts[::2], parts[1::2])]
        return acc + parts[0].astype(F32)

    acc = lax.fori_loop(0, n_steps, body, jnp.zeros((16, width), F32))
    return acc.sum(axis=0, keepdims=True)


def _kth_largest_score(score_ref, round_ref, n_chunks, chunk, topk):
    width = score_ref.shape[1]
    n_steps = (n_chunks + COUNT_WIDE - 1) // COUNT_WIDE
    rows = COUNT_WIDE * chunk

    def coarse_step(k, p16):
        cand = p16 + jnp.left_shift(jnp.int32(1), 15 - k)
        bits16 = (cand ^ ((cand >> 31) & 0x7FFF)) & 0xFFFF
        cand_b = pltpu.bitcast(bits16 << 16, F32).astype(BF16)
        cnt = _count_rounded(round_ref, n_steps, rows, cand_b)
        return jnp.where(cnt >= topk, cand, p16)

    p16 = lax.fori_loop(0, 16, coarse_step, jnp.full((1, width), -32768, I32))
    few = p16 == -32768
    below = jnp.where(few, 0, p16 - 1)
    base = jnp.where(below >= 0, below * 65536, below * 65536 + 65535)

    def fine_step(k, off):
        cand_off = off | jnp.left_shift(jnp.int32(1), 16 - k)
        cand_f = _pattern_to_float(base + cand_off)
        cnt = _count_rows(score_ref, n_steps, rows, lambda s, idx: s >= cand_f)
        return jnp.where(cnt >= topk, cand_off, off)

    off = lax.fori_loop(0, 17, fine_step, jnp.zeros((1, width), I32))
    t = jnp.where(few, NEG_INF, _pattern_to_float(base + off))
    n_gt = _count_rows(score_ref, n_steps, rows, lambda s, idx: s > t)
    return t, jnp.where(few, 0, topk - n_gt)


def _tie_cutoff(score_ref, n_chunks, chunk, t, need, idx_bits):
    width = score_ref.shape[1]
    n_steps = (n_chunks + COUNT_WIDE - 1) // COUNT_WIDE
    rows = COUNT_WIDE * chunk

    def idx_step(k, c):
        cand = c + jnp.left_shift(jnp.int32(1), idx_bits - 1 - k)
        cnt = _count_rows(score_ref, n_steps, rows,
                          lambda s, idx: jnp.logical_and(s == t, idx < cand))
        return jnp.where(cnt < need, cand, c)

    c = lax.fori_loop(0, idx_bits, idx_step, jnp.zeros((1, width), I32))
    return jnp.where(need > 0, c, -1)


def _dsa_kernel(qi_ref, kiw_ref, kei_ref, qd_ref, ked_ref, vdt_ref, o_ref,
                score_ref, round_ref, acc_ref, kmax_ref, *, T, CK, topk):
    i = pl.program_id(1)
    n_full = (i * T) // CK
    n_chunks = n_full + 1
    rr = lax.broadcasted_iota(I32, (CK, T), 0)
    cc = lax.broadcasted_iota(I32, (CK, T), 1)
    w_t = kiw_ref[...].T[IDX_DIM:IDX_DIM + IDX_HEADS, :]

    def score_chunk(j, last):
        start = pl.multiple_of(j * CK, CK)
        ka = kei_ref[pl.ds(start, CK), :LANES]
        kb = kei_ref[pl.ds(start, CK), LANES:]
        prods = []
        for p in range(IDX_HEADS // 2):
            qp = qi_ref[:, p * LANES:(p + 1) * LANES]
            prods += [_dot_nt(ka, qp), _dot_nt(kb, qp)]
        sc = jnp.zeros((CK, T), F32)
        for h, s in enumerate(prods):
            sc = sc + w_t[h:h + 1, :] * jnp.maximum(s, 0.0)
        sc = _canonical_score(sc * IDX_SCALE)
        if last:
            sc = jnp.where(start + rr <= i * T + cc, sc, NEG_INF)
        score_ref[pl.ds(start, CK), :] = sc
        round_ref[pl.ds(start, CK), :] = sc.astype(BF16)

    def score_body(j, carry):
        score_chunk(j, False)
        return carry

    lax.fori_loop(0, n_full, score_body, 0)
    score_chunk(n_full, True)

    @pl.when(n_chunks % COUNT_WIDE != 0)
    def _():
        pad_start = pl.multiple_of(n_chunks * CK, CK)
        score_ref[pl.ds(pad_start, CK), :] = jnp.full((CK, T), NEG_INF, F32)
        round_ref[pl.ds(pad_start, CK), :] = jnp.full((CK, T), NEG_INF, BF16)

    t, need = _kth_largest_score(score_ref, round_ref, n_chunks, CK, topk)
    need_f = need.astype(F32)

    acc_ref[...] = jnp.zeros(acc_ref.shape, F32)
    top_rows = lax.broadcasted_iota(I32, (LANES, T), 0) < HEAD_DIM
    r2 = lax.broadcasted_iota(I32, (CK, CK), 0)
    c2 = lax.broadcasted_iota(I32, (CK, CK), 1)
    prefix = (c2 <= r2).astype(BF16)

    def chunk_inputs(j, tied_before):
        start = pl.multiple_of(j * CK, CK)
        sc = score_ref[pl.ds(start, CK), :]
        tied = sc == t
        rank = tied_before + _dot(prefix, jnp.where(tied, 1.0, 0.0).astype(BF16))
        sel = jnp.logical_or(sc > t, jnp.logical_and(tied, rank <= need_f))
        return start, sel, rank[CK - 1:CK, :]

    def pair_inputs(j, start, p):
        kvh = p // 2
        kcol = kvh * 2 * LANES
        qp = qd_ref[:, p * LANES:(p + 1) * LANES]
        vt = vdt_ref[j, kvh * HEAD_DIM:(kvh + 1) * HEAD_DIM, :]
        kblks = [ked_ref[pl.ds(start, CK), kcol + e * LANES:kcol + (e + 1) * LANES] for e in range(2)]
        return qp, vt, kblks

    def attend_online(j, state):
        ms, ls, tied_before = state
        start, sel, tied_before = chunk_inputs(j, tied_before)
        new_ms, new_ls = [], []
        for p in range(DSA_HEADS // 2):
            qp, vt, kblks = pair_inputs(j, start, p)
            outs, alphas = [], []
            for e in range(2):
                h = 2 * p + e
                lg = jnp.where(sel, _dot_nt(kblks[e], qp), NEG_BIG)
                m_new = jnp.maximum(ms[h], jnp.max(lg, axis=0, keepdims=True))
                alpha = jnp.exp(ms[h] - m_new)
                pr = jnp.where(sel, jnp.exp(lg - m_new), 0.0)
                new_ls.append(alpha * ls[h] + jnp.sum(pr, axis=0, keepdims=True))
                new_ms.append(m_new)
                outs.append(_dot(vt, pr.astype(BF16)))
                alphas.append(alpha)
            scale = jnp.where(top_rows, alphas[0], alphas[1])
            acc_ref[p] = acc_ref[p] * scale + jnp.concatenate(outs, axis=0)
        return tuple(new_ms), tuple(new_ls), tied_before

    def softmax_online(_):
        init = (tuple(jnp.full((1, T), NEG_BIG, F32) for _ in range(DSA_HEADS)),
                tuple(jnp.zeros((1, T), F32) for _ in range(DSA_HEADS)),
                jnp.zeros((1, T), F32))
        return lax.fori_loop(0, n_chunks, attend_online, init)[1]

    def attend_unshifted(j, state):
        ls, tied_before = state
        start, sel, tied_before = chunk_inputs(j, tied_before)
        pairs = [pair_inputs(j, start, p) for p in range(DSA_HEADS // 2)]
        lgs = [_dot_nt(kblks[e], qp) for qp, _, kblks in pairs for e in range(2)]
        prs = [jnp.where(sel, jnp.exp(lg), 0.0) for lg in lgs]
        new_ls = tuple(l + pr.reshape(CK // 8, 8, T).sum(axis=0) for l, pr in zip(ls, prs))
        for p, (_, vt, _) in enumerate(pairs):
            outs = [_dot(vt, prs[2 * p + e].astype(BF16)) for e in range(2)]
            acc_ref[p] = acc_ref[p] + jnp.concatenate(outs, axis=0)
        return new_ls, tied_before

    def softmax_unshifted(_):
        init = (tuple(jnp.zeros((8, T), F32) for _ in range(DSA_HEADS)), jnp.zeros((1, T), F32))
        ls = lax.fori_loop(0, n_chunks, attend_unshifted, init)[0]
        return tuple(l.sum(axis=0, keepdims=True) for l in ls)

    @pl.when(i == 0)
    def _():
        for kvh in range(DSA_KV_HEADS):
            def k_norm(c, m):
                rows = ked_ref[pl.ds(pl.multiple_of(c * CK, CK), CK),
                               kvh * 2 * LANES:kvh * 2 * LANES + LANES].astype(F32)
                return jnp.maximum(m, jnp.max(jnp.sum(rows * rows, axis=1, keepdims=True)))
            kmax_ref[kvh] = lax.fori_loop(0, ked_ref.shape[0] // CK, k_norm, jnp.float32(0.0))

    bound2 = jnp.float32(0.0)
    for p in range(DSA_HEADS // 2):
        qf = qd_ref[:, p * LANES:(p + 1) * LANES].astype(F32)
        q2max = jnp.max(jnp.sum(qf * qf, axis=1, keepdims=True))
        bound2 = jnp.maximum(bound2, q2max * kmax_ref[p // 2])
    ls = lax.cond(bound2 <= LOGIT_SAFE * LOGIT_SAFE, softmax_unshifted, softmax_online, 0)
    for p in range(DSA_HEADS // 2):
        denom = jnp.where(top_rows, ls[2 * p], ls[2 * p + 1])
        o_ref[:, p * LANES:(p + 1) * LANES] = (acc_ref[p] / denom).T.astype(o_ref.dtype)


def _dsa_attention(qi, kiw, kei, qd, ked, vdt, B, S, T, topk):
    nq = S // T
    CK = vdt.shape[2]
    assert S % (COUNT_WIDE * CK) == 0 and CK % T == 0
    qrow = lambda b, i: (b * nq + i, 0)
    batch = lambda b, i: (b, 0)
    return pl.pallas_call(
        functools.partial(_dsa_kernel, T=T, CK=CK, topk=topk),
        grid=(B, nq),
        in_specs=[
            pl.BlockSpec((T, 512), qrow),
            pl.BlockSpec((T, LANES), qrow),
            pl.BlockSpec((S, 256), batch),
            pl.BlockSpec((T, 512), qrow),
            pl.BlockSpec((S, 512), batch),
            pl.BlockSpec((S // CK, LANES, CK), lambda b, i: (b, 0, 0)),
        ],
        out_specs=pl.BlockSpec((T, 512), qrow),
        out_shape=jax.ShapeDtypeStruct((B * S, 512), BF16),
        scratch_shapes=[
            pltpu.VMEM((S, T), F32),
            pltpu.VMEM((S, T), BF16),
            pltpu.VMEM((DSA_HEADS // 2, LANES, T), F32),
            pltpu.SMEM((DSA_KV_HEADS,), F32),
        ],
        compiler_params=pltpu.CompilerParams(dimension_semantics=("arbitrary", "arbitrary"),
                                             vmem_limit_bytes=VMEM_LIMIT),
        name="dsa_prompt",
    )(qi, kiw, kei, qd, ked, vdt)


def _merge_kernel(x_ref, osb_ref, zsb_ref, od_ref, zd_ref, gsb_ref, gd_ref,
                  wsb_ref, wd_ref, wo_ref, y_ref):
    zsb = zsb_ref[...].astype(F32)
    zd = zd_ref[...].astype(F32)
    a_sb = (osb_ref[...] * (zsb * jax.nn.sigmoid(zsb))).astype(BF16)
    a_d = (od_ref[...] * (zd * jax.nn.sigmoid(zd))).astype(BF16)
    u_sb = _dot(a_sb, wsb_ref[...])
    u_d = _dot(a_d, wd_ref[...])
    mixed = (jax.nn.sigmoid(gsb_ref[...].astype(F32)) * u_sb
             + jax.nn.sigmoid(gd_ref[...].astype(F32)) * u_d)
    y_ref[...] = x_ref[...] + _dot(mixed.astype(BF16), wo_ref[...])


def _merge(x2d, osb, zsb, od, zd, gsb, gd, wsb, wd, wo, tm):
    R = x2d.shape[0]
    row = lambda i: (i, 0)
    const = lambda i: (0, 0)
    return pl.pallas_call(
        _merge_kernel,
        grid=(R // tm,),
        in_specs=[
            pl.BlockSpec((tm, D_MODEL), row),
            pl.BlockSpec((tm, 512), row),
            pl.BlockSpec((tm, 512), row),
            pl.BlockSpec((tm, 512), row),
            pl.BlockSpec((tm, 512), row),
            pl.BlockSpec((tm, D_MODEL), row),
            pl.BlockSpec((tm, D_MODEL), row),
            pl.BlockSpec((512, D_MODEL), const),
            pl.BlockSpec((512, D_MODEL), const),
            pl.BlockSpec((D_MODEL, D_MODEL), const),
        ],
        out_specs=pl.BlockSpec((tm, D_MODEL), row),
        out_shape=jax.ShapeDtypeStruct((R, D_MODEL), F32),
        compiler_params=pltpu.CompilerParams(dimension_semantics=("arbitrary",),
                                             vmem_limit_bytes=VMEM_LIMIT),
        name="merge",
    )(x2d, osb, zsb, od, zd, gsb, gd, wsb, wd, wo)


def _rope_tables(pos):
    inv = ROPE_THETA ** (-jnp.arange(HALF, dtype=F32) / HALF)
    ang = pos.astype(F32)[:, None] * inv[None, :]
    cos = jnp.cos(ang)
    sin = jnp.sin(ang)
    return jnp.tile(cos, (1, 4)), jnp.tile(jnp.concatenate([-sin, sin], axis=1), (1, 2))


def _prep_weights(norm_g, w_in, q_norm_g, k_norm_g, idx_k_norm_g, w_up_sb, w_up_dsa, w_out):
    w_r = (jnp.pad(w_in[:, :_W_SPLIT].astype(BF16), ((0, 0), (0, _C_GSB - _W_SPLIT))),
           w_in[:, _W_SPLIT:].astype(BF16))
    qn = jnp.tile(q_norm_g, 2)[None, :]
    kn = jnp.tile(k_norm_g, 2)[None, :]
    inn = jnp.concatenate([idx_k_norm_g, jnp.zeros((LANES - IDX_DIM,), F32)])[None, :]
    return (norm_g[None, :], w_r, qn, kn, inn,
            w_up_sb.astype(BF16), w_up_dsa.astype(BF16), w_out.astype(BF16))


def _prompt_layer(x, weights, tiles):
    B, S, _ = x.shape
    t_feat, t_merge, t_sb, t_dsa, key_chunk = tiles
    norm_g, w_r, qn, kn, inn, wsb, wd, wo = weights
    x2d = x.reshape(B * S, D_MODEL)
    cos, sin = _rope_tables(jnp.arange(S))
    f = _features(x2d, S, cos, sin, norm_g, w_r, qn, kn, inn, t_feat, key_chunk, True)
    topk = min(DSA_TOPK_MAX, S // 4)
    osb = _sb_attention(f["qsb"], f["kesb"], f["vesb"], B, S, t_sb)
    od = _dsa_attention(f["qi"], f["kiw"], f["kei"], f["qd"], f["ked"], f["vdt"], B, S, t_dsa, topk)
    y = _merge(x2d, osb, f["zsb"], od, f["zd"], f["gsb"], f["gd"], wsb, wd, wo, t_merge)

    def rows(a, heads):
        return jnp.moveaxis(a.reshape(B, heads, HEAD_DIM, S), 3, 1)[None]

    return (y.reshape(B, S, D_MODEL),
            rows(f["ksb"], SB_KV_HEADS), rows(f["vsb"], SB_KV_HEADS),
            rows(f["kd"], DSA_KV_HEADS), rows(f["vd"], DSA_KV_HEADS),
            jnp.moveaxis(f["ki"], 2, 1)[None])


PAGE_UNROLL = 32


def _page_copy(cache_hbm, pt_ref, buf, sems, ci, seq, slot, p):
    return pltpu.make_async_copy(cache_hbm.at[pt_ref[seq, p]], buf.at[slot, p], sems.at[ci, slot])


def _stream_pages(caches, pt_ref, bufs, sems, n_pages):
    b = pl.program_id(0)
    nb = pl.num_programs(0)
    slot = b % 2

    def start_all_pages(seq, slot_):
        def body(p, carry):
            for ci, (cache, buf) in enumerate(zip(caches, bufs)):
                _page_copy(cache, pt_ref, buf, sems, ci, seq, slot_, p).start()
            return carry
        lax.fori_loop(0, n_pages, body, 0, unroll=min(8, n_pages))

    @pl.when(b == 0)
    def _():
        start_all_pages(0, 0)

    @pl.when(b + 1 < nb)
    def _():
        start_all_pages(b + 1, 1 - slot)

    for p in range(n_pages):
        for ci, (cache, buf) in enumerate(zip(caches, bufs)):
            _page_copy(cache, pt_ref, buf, sems, ci, b, slot, p).wait()
    return slot


def _dec_sb_kernel(pt_ref, qm_ref, k_hbm, v_hbm, o_ref, kfirst, vfirst, kmore, vmore, sems, *, n_pages, G):
    b = pl.program_id(0)
    nb = pl.num_programs(0)
    slot = b % 2
    n_groups = n_pages // G
    qm = qm_ref[0]
    P = PAGE_SIZE
    rr = lax.broadcasted_iota(I32, (P, P), 0)
    cc = lax.broadcasted_iota(I32, (P, P), 1)
    suffix = (rr > cc).astype(BF16)
    ones = jnp.ones((P, LANES), BF16)

    def group_copies(seq, g, kdst, vdst, ksem, vsem):
        copies = []
        for r in range(G):
            page = pt_ref[seq, n_pages - (g + 1) * G + r]
            copies.append(pltpu.make_async_copy(k_hbm.at[page], kdst.at[r], ksem))
            copies.append(pltpu.make_async_copy(v_hbm.at[page], vdst.at[r], vsem))
        return copies

    def first_copies(seq, s):
        return group_copies(seq, 0, kfirst.at[s], vfirst.at[s], sems.at[0, s], sems.at[1, s])

    def process(kb, vb, carry, acc):
        zs, l1ms, tails, tots = [], [], [], []
        for r in range(G):
            z = _dot(qm, kb[r].astype(BF16))
            l1m = -_sb_softplus_parts(z)
            hi, lo = _split_bf16(l1m)
            zs.append(z)
            l1ms.append(l1m)
            tails.append(_dot(hi, suffix) + _dot(lo, suffix))
            tots.append(_dot(hi, ones) + _dot(lo, ones))
        for r in reversed(range(G)):
            a = jnp.exp(zs[r] + l1ms[r] + tails[r] + carry).astype(BF16)
            acc = acc + _dot_nt(a, vb[r].astype(BF16))
            carry = carry + tots[r]
        return carry, acc

    @pl.when(b == 0)
    def _():
        for c in first_copies(0, 0):
            c.start()

    @pl.when(b + 1 < nb)
    def _():
        for c in first_copies(b + 1, 1 - slot):
            c.start()

    for c in first_copies(b, slot):
        c.wait()
    n_heads = qm.shape[0]
    carry, acc = process(kfirst.at[slot], vfirst.at[slot],
                         jnp.zeros((n_heads, LANES), F32), jnp.zeros((n_heads, qm.shape[1]), F32))

    def cond(s):
        return jnp.logical_and(s[0] < n_groups, s[3] >= SB_DEAD)

    def body(s):
        g, carry, acc, _ = s
        copies = group_copies(b, g, kmore, vmore, sems.at[2, 0], sems.at[2, 1])
        for c in copies:
            c.start()
        for c in copies:
            c.wait()
        carry, acc = process(kmore, vmore, carry, acc)
        return g + 1, carry, acc, jnp.max(carry)

    _, _, acc, _ = lax.while_loop(cond, body, (jnp.int32(1), carry, acc, jnp.max(carry)))
    o_ref[0] = acc


def _dec_sb(page_table, qm, cache_k, cache_v, group):
    DB, n_pages = page_table.shape
    width = cache_k.shape[1]
    return pl.pallas_call(
        functools.partial(_dec_sb_kernel, n_pages=n_pages, G=group),
        grid_spec=pltpu.PrefetchScalarGridSpec(
            num_scalar_prefetch=1,
            grid=(DB,),
            in_specs=[
                pl.BlockSpec((1, SB_HEADS, width), lambda b, pt: (b, 0, 0)),
                pl.BlockSpec(memory_space=pl.ANY),
                pl.BlockSpec(memory_space=pl.ANY),
            ],
            out_specs=pl.BlockSpec((1, SB_HEADS, width), lambda b, pt: (b, 0, 0)),
            scratch_shapes=[
                pltpu.VMEM((2, group, width, PAGE_SIZE), F32),
                pltpu.VMEM((2, group, width, PAGE_SIZE), F32),
                pltpu.VMEM((group, width, PAGE_SIZE), F32),
                pltpu.VMEM((group, width, PAGE_SIZE), F32),
                pltpu.SemaphoreType.DMA((3, 2)),
            ]),
        out_shape=jax.ShapeDtypeStruct((DB, SB_HEADS, width), F32),
        compiler_params=pltpu.CompilerParams(dimension_semantics=("arbitrary",),
                                             vmem_limit_bytes=VMEM_LIMIT),
        name="dec_sb",
    )(page_table, qm, cache_k, cache_v)


def _dec_idx_kernel(pt_ref, qi_ref, w_ref, kinew_ref, k_hbm, score_ref, kbuf, sems, *, n_pages, n_rows):
    slot = _stream_pages((k_hbm,), pt_ref, (kbuf,), sems, n_pages)
    qi = qi_ref[0]
    w = w_ref[0]

    def score(pp, carry):
        kk = jnp.concatenate([kbuf[slot, 2 * pp], kbuf[slot, 2 * pp + 1]], axis=1).astype(BF16)
        s = _dot(qi, kk)
        sc = _canonical_score(jnp.sum(w * jnp.maximum(s, 0.0), axis=0, keepdims=True) * IDX_SCALE)
        score_ref[0, pl.ds(2 * pp, 1), :] = sc[:, :PAGE_SIZE]
        score_ref[0, pl.ds(2 * pp + 1, 1), :] = sc[:, PAGE_SIZE:]
        return carry

    lax.fori_loop(0, n_pages // 2, score, 0, unroll=min(PAGE_UNROLL // 2, n_pages // 2))
    s_new = jnp.sum(qi.astype(F32) * kinew_ref[0].astype(F32), axis=1, keepdims=True)
    sc_new = jnp.sum(w * jnp.maximum(s_new, 0.0), axis=0, keepdims=True) * IDX_SCALE
    tail_rows = n_rows - n_pages
    r = lax.broadcasted_iota(I32, (tail_rows, LANES), 0)
    c = lax.broadcasted_iota(I32, (tail_rows, LANES), 1)
    new_row = jnp.broadcast_to(_canonical_score(sc_new), (tail_rows, LANES))
    score_ref[0, n_pages:, :] = jnp.where(jnp.logical_and(r == 0, c == 0), new_row, NEG_INF)


def _dec_idx(page_table, qi, w, kinew, cache_k, n_rows):
    DB, n_pages = page_table.shape
    per_seq = lambda b, pt: (b, 0, 0)
    return pl.pallas_call(
        functools.partial(_dec_idx_kernel, n_pages=n_pages, n_rows=n_rows),
        grid_spec=pltpu.PrefetchScalarGridSpec(
            num_scalar_prefetch=1,
            grid=(DB,),
            in_specs=[
                pl.BlockSpec((1, IDX_HEADS, IDX_DIM), per_seq),
                pl.BlockSpec((1, IDX_HEADS, 1), per_seq),
                pl.BlockSpec((1, 1, IDX_DIM), per_seq),
                pl.BlockSpec(memory_space=pl.ANY),
            ],
            out_specs=pl.BlockSpec((1, n_rows, LANES), per_seq),
            scratch_shapes=[
                pltpu.VMEM((2, n_pages, IDX_DIM, PAGE_SIZE), F32),
                pltpu.SemaphoreType.DMA((1, 2)),
            ]),
        out_shape=jax.ShapeDtypeStruct((DB, n_rows, LANES), F32),
        compiler_params=pltpu.CompilerParams(dimension_semantics=("arbitrary",),
                                             vmem_limit_bytes=VMEM_LIMIT),
        name="dec_idx",
    )(page_table, qi, w, kinew, cache_k)


def _dec_thr_kernel(scores_ref, t_ref, c_ref, score_scr, round_scr, *, n_rows, topk, idx_bits):
    for p in range(n_rows):
        blk = scores_ref[:, p * LANES:(p + 1) * LANES].T
        score_scr[p * LANES:(p + 1) * LANES, :] = blk
        round_scr[p * LANES:(p + 1) * LANES, :] = blk.astype(BF16)
    n_chunks, chunk = n_rows // 2, 2 * LANES
    t, need = _kth_largest_score(score_scr, round_scr, n_chunks, chunk, topk)
    t_ref[...] = t
    c_ref[...] = _tie_cutoff(score_scr, n_chunks, chunk, t, need, idx_bits)


def _dec_thr(scores2d, n_rows, topk):
    DB = scores2d.shape[0]
    assert n_rows % (2 * COUNT_WIDE) == 0
    idx_bits = (n_rows * LANES - 1).bit_length()
    return pl.pallas_call(
        functools.partial(_dec_thr_kernel, n_rows=n_rows, topk=topk, idx_bits=idx_bits),
        out_shape=[jax.ShapeDtypeStruct((1, DB), F32), jax.ShapeDtypeStruct((1, DB), I32)],
        scratch_shapes=[pltpu.VMEM((n_rows * LANES, DB), F32), pltpu.VMEM((n_rows * LANES, DB), BF16)],
        compiler_params=pltpu.CompilerParams(vmem_limit_bytes=VMEM_LIMIT),
        name="dec_thr",
    )(scores2d)


def _dec_dsa_kernel(pt_ref, c_ref, t_ref, qm_ref, scores_ref, kdnew_ref, vdnew_ref, k_hbm, v_hbm, o_ref,
                    kbuf, vbuf, lbuf, sems, *, n_pages):
    b = pl.program_id(0)
    slot = _stream_pages((k_hbm, v_hbm), pt_ref, (kbuf, vbuf), sems, n_pages)
    qm = qm_ref[0]
    t = t_ref[0]
    cidx = c_ref[b]
    lane = lax.broadcasted_iota(I32, (1, LANES), 1)

    def selected(p):
        sc = scores_ref[0, pl.ds(p, 1), :]
        idx = p * PAGE_SIZE + lane
        return jnp.logical_or(sc > t, jnp.logical_and(sc == t, idx <= cidx))

    def page_pair(buf, pp):
        return jnp.concatenate([buf[slot, 2 * pp], buf[slot, 2 * pp + 1]], axis=1).astype(BF16)

    def logits(pp, m):
        sel = jnp.concatenate([selected(2 * pp), selected(2 * pp + 1)], axis=1)
        lg = jnp.where(sel, _dot(qm, page_pair(kbuf, pp)), NEG_BIG)
        lbuf[pp] = lg
        return jnp.maximum(m, lg)

    n_heads = qm.shape[0]
    n_pairs = n_pages // 2
    unroll = min(PAGE_UNROLL // 2, n_pairs)
    m = lax.fori_loop(0, n_pairs, logits, jnp.full((n_heads, 2 * LANES), NEG_BIG, F32), unroll=unroll)
    sel_new = selected(n_pages)[:, :1]
    lg_new = jnp.sum(qm.astype(F32) * kdnew_ref[0].astype(F32), axis=1, keepdims=True)
    lg_new = jnp.where(sel_new, lg_new, NEG_BIG)
    m = jnp.maximum(jnp.max(m, axis=1, keepdims=True), lg_new)

    def attend(pp, state):
        l, acc = state
        pr = jnp.exp(lbuf[pp] - m)
        return l + pr, acc + _dot_nt(pr.astype(BF16), page_pair(vbuf, pp))

    l, acc = lax.fori_loop(0, n_pairs, attend,
                           (jnp.zeros((n_heads, 2 * LANES), F32), jnp.zeros((n_heads, LANES), F32)),
                           unroll=unroll)
    pr_new = jnp.where(sel_new, jnp.exp(lg_new - m), 0.0)
    pr_new_b = pr_new.astype(BF16).astype(F32)
    acc = acc + pr_new_b * vdnew_ref[0].astype(F32)
    denom = jnp.sum(l, axis=1, keepdims=True) + pr_new
    o_ref[0] = acc / denom


def _dec_dsa(page_table, cidx, t_rep, qm, scores, kdnew, vdnew, cache_k, cache_v):
    DB, n_pages = page_table.shape
    n_rows = scores.shape[1]
    per_seq = lambda b, pt, c: (b, 0, 0)
    return pl.pallas_call(
        functools.partial(_dec_dsa_kernel, n_pages=n_pages),
        grid_spec=pltpu.PrefetchScalarGridSpec(
            num_scalar_prefetch=2,
            grid=(DB,),
            in_specs=[
                pl.BlockSpec((1, 1, LANES), per_seq),
                pl.BlockSpec((1, DSA_HEADS, LANES), per_seq),
                pl.BlockSpec((1, n_rows, LANES), per_seq),
                pl.BlockSpec((1, 1, LANES), per_seq),
                pl.BlockSpec((1, 1, LANES), per_seq),
                pl.BlockSpec(memory_space=pl.ANY),
                pl.BlockSpec(memory_space=pl.ANY),
            ],
            out_specs=pl.BlockSpec((1, DSA_HEADS, LANES), per_seq),
            scratch_shapes=[
                pltpu.VMEM((2, n_pages, PAGE_SIZE, LANES), F32),
                pltpu.VMEM((2, n_pages, PAGE_SIZE, LANES), F32),
                pltpu.VMEM((n_pages // 2, DSA_HEADS, 2 * LANES), F32),
                pltpu.SemaphoreType.DMA((2, 2)),
            ]),
        out_shape=jax.ShapeDtypeStruct((DB, DSA_HEADS, LANES), F32),
        compiler_params=pltpu.CompilerParams(dimension_semantics=("arbitrary",),
                                             vmem_limit_bytes=VMEM_LIMIT),
        name="dec_dsa",
    )(page_table, cidx, t_rep, qm, scores, kdnew, vdnew, cache_k, cache_v)


def _heads_in_kv_lanes(q, n_heads, n_kv):
    R = q.shape[0]
    q3 = q.reshape(R, n_heads, 1, HEAD_DIM)
    kv_of_head = jnp.arange(n_heads) // (n_heads // n_kv)
    onehot = (kv_of_head[:, None] == jnp.arange(n_kv)[None, :])[None, :, :, None]
    return jnp.where(onehot, q3, jnp.zeros((), q.dtype)).reshape(R, n_heads, n_kv * HEAD_DIM)


def _own_kv_lanes(o, n_heads, n_kv):
    R = o.shape[0]
    o4 = o.reshape(R, n_heads, n_kv, HEAD_DIM)
    group = n_heads // n_kv
    return jnp.concatenate([o4[:, h, h // group, :] for h in range(n_heads)], axis=1)


def _sample_layer(x, caches, page_table, weights):
    DB, T, _ = x.shape
    c_sb_k, c_sb_v, c_dsa_k, c_dsa_v, c_idx_k = caches
    norm_g, w_r, qn, kn, inn, wsb, wd, wo = weights
    n_pages = page_table.shape[1]
    past = n_pages * PAGE_SIZE
    x2d = x.reshape(DB * T, D_MODEL)
    cos, sin = _rope_tables(jnp.full((DB * T,), past))
    f = _features(x2d, DB * T, cos, sin, norm_g, w_r, qn, kn, inn, DB * T, DB * T, False)

    n_pool = c_sb_k.shape[0]

    def pages_t(c):
        return jnp.moveaxis(c, 1, -1).reshape(n_pool, -1, PAGE_SIZE)

    qm_sb = _heads_in_kv_lanes(f["qsb"], SB_HEADS, SB_KV_HEADS)
    o_sb = _dec_sb(page_table, qm_sb, pages_t(c_sb_k), pages_t(c_sb_v), 2)
    osb = _own_kv_lanes(o_sb, SB_HEADS, SB_KV_HEADS)

    n_rows = n_pages + 8
    qi3 = f["qi"].reshape(DB, IDX_HEADS, IDX_DIM)
    w3 = f["kiw"][:, IDX_DIM:IDX_DIM + IDX_HEADS].reshape(DB, IDX_HEADS, 1)
    kinew = f["kei"][:, :IDX_DIM].reshape(DB, 1, IDX_DIM)
    scores = _dec_idx(page_table, qi3, w3, kinew, pages_t(c_idx_k), n_rows)
    topk = min(DSA_TOPK_MAX, (past + T) // 4)
    t, cidx = _dec_thr(scores.reshape(DB, n_rows * LANES), n_rows, topk)
    t_rep = jnp.broadcast_to(t.reshape(DB, 1, 1), (DB, 1, LANES))

    qm_d = _heads_in_kv_lanes(f["qd"], DSA_HEADS, DSA_KV_HEADS)
    kdnew = f["kd"].astype(BF16).reshape(DB, 1, LANES)
    vdnew = f["vd"].astype(BF16).reshape(DB, 1, LANES)
    o_d = _dec_dsa(page_table, cidx.reshape(DB), t_rep, qm_d, scores, kdnew, vdnew,
                   pages_t(c_dsa_k), pages_t(c_dsa_v))
    od = _own_kv_lanes(o_d, DSA_HEADS, DSA_KV_HEADS)

    y = _merge(x2d, osb, f["zsb"], od, f["zd"], f["gsb"], f["gd"], wsb, wd, wo, DB * T)
    return (y.reshape(DB, T, D_MODEL),
            f["ksb"].reshape(1, DB, T, SB_KV_HEADS, HEAD_DIM),
            f["vsb"].reshape(1, DB, T, SB_KV_HEADS, HEAD_DIM),
            f["kd"].reshape(1, DB, T, DSA_KV_HEADS, HEAD_DIM),
            f["vd"].reshape(1, DB, T, DSA_KV_HEADS, HEAD_DIM),
            f["ki"].reshape(1, DB, T, IDX_DIM))


def kernel(x_prompt, x_sample, cache_sb_k, cache_sb_v, cache_dsa_k, cache_dsa_v, cache_idx_k, page_table,
           norm_g, w_in, q_norm_g, k_norm_g, idx_k_norm_g, w_up_sb, w_up_dsa, w_out):
    assert norm_g.shape[0] == 1 and x_sample.shape[1] == 1
    weights = _prep_weights(norm_g[0], w_in[0], q_norm_g[0], k_norm_g[0], idx_k_norm_g[0],
                            w_up_sb[0], w_up_dsa[0], w_out[0])
    p = _prompt_layer(x_prompt, weights, PROMPT_TILES)
    caches = (cache_sb_k[0], cache_sb_v[0], cache_dsa_k[0], cache_dsa_v[0], cache_idx_k[0])
    s = _sample_layer(x_sample, caches, page_table, weights)
    return (p[0], s[0], p[1], p[2], p[3], p[4], p[5], s[1], s[2], s[3], s[4], s[5])
```

```python
import functools

import jax
import jax.numpy as jnp
from jax import lax
from jax.experimental import pallas as pl
from jax.experimental.pallas import tpu as pltpu

F32 = jnp.float32
BF16 = jnp.bfloat16
I32 = jnp.int32
I16 = jnp.int16

LANES = 128
HEAD_DIM = 64
HALF = HEAD_DIM // 2
SB_HEADS, SB_KV_HEADS = 8, 4
DSA_HEADS, DSA_KV_HEADS = 8, 2
IDX_HEADS, IDX_DIM = 8, 64
D_MODEL = 1024
PAGE_SIZE = 128
DSA_TOPK_MAX = 256
ROPE_THETA = 10000.0
NORM_EPS = 1e-6
ATTN_SCALE = HEAD_DIM ** -0.5
IDX_SCALE = (IDX_HEADS * IDX_DIM) ** -0.5
INT_MIN = -2 ** 31
NEG_BIG = -1e30
SB_DEAD = -110.0
LOGIT_SAFE = 40.0
VMEM_LIMIT = 56 * 1024 * 1024
PROMPT_TILES = (256, 256, 128, 256, 256)

_C_QSB, _C_KSB, _C_VSB, _C_ZSB = 0, 512, 768, 1024
_C_QD, _C_KD, _C_VD, _C_ZD = 1536, 2048, 2176, 2304
_C_QI, _C_KI, _C_GSB, _C_GD, _C_END = 2816, 3328, 3456, 4480, 5504
_W_SPLIT = 3400


def _dot(a, b):
    return jnp.dot(a, b, preferred_element_type=F32)


def _dot_nt(a, b):
    return lax.dot_general(a, b, (((1,), (1,)), ((), ())), preferred_element_type=F32)


def _dot_tn(a, b):
    return lax.dot_general(a, b, (((0,), (0,)), ((), ())), preferred_element_type=F32)


def _split_bf16(x):
    hi = x.astype(BF16)
    lo = (x - hi.astype(F32)).astype(BF16)
    return hi, lo


def _feat_kernel(x_ref, g_ref, wa_ref, wb_ref, cos_ref, sin_ref, qn_ref, kn_ref, in_ref,
                 ksb_ref, vsb_ref, kd_ref, vd_ref, ki_ref,
                 qsb_ref, zsb_ref, qd_ref, zd_ref, qi_ref, kiw_ref, gsb_ref, gd_ref,
                 kesb_ref, vesb_ref, ked_ref, kei_ref, vdt_ref, *, feature_major):
    def put_rows(ref, val):
        if feature_major:
            ref[0] = val.T
        else:
            ref[...] = val

    x = x_ref[...]
    ms = jnp.mean(x * x, axis=-1, keepdims=True)
    xn = (x * lax.rsqrt(ms + NORM_EPS) * g_ref[...]).astype(BF16)

    tm = x.shape[0]
    lane = lax.broadcasted_iota(I32, (tm, LANES), 1)
    lo_half = lane < HEAD_DIM
    lo_rot = (lane % HEAD_DIM) < HALF
    r = lax.broadcasted_iota(I32, (LANES, LANES), 0) // HEAD_DIM
    c = lax.broadcasted_iota(I32, (LANES, LANES), 1) // HEAD_DIM
    group_ones = (r == c).astype(BF16)
    group_ones2 = jnp.concatenate([group_ones, group_ones], axis=0)
    cos = cos_ref[...]
    sin = sin_ref[...]

    def proj(c0, c1):
        if c1 <= _C_GSB:
            return _dot(xn, wa_ref[:, c0:c1])
        return _dot(xn, wb_ref[:, c0 - _C_GSB:c1 - _C_GSB])

    def head_norm(v, gain):
        ss = _dot(jnp.concatenate(_split_bf16(v * v), axis=1), group_ones2)
        return v * lax.rsqrt(ss * (1.0 / HEAD_DIM) + NORM_EPS) * gain

    def rope(v):
        rot = jnp.where(lo_rot, pltpu.roll(v, LANES - HALF, 1), pltpu.roll(v, HALF, 1))
        return v * cos + rot * sin

    def expand(pair, e):
        if e == 0:
            a = jnp.where(lo_half, pair, 0.0)
            return a, pltpu.roll(a, HEAD_DIM, 1)
        b = jnp.where(lo_half, 0.0, pair)
        return pltpu.roll(b, HEAD_DIM, 1), b

    def store_expanded(dst_ref, val, n_heads):
        for h in range(n_heads):
            pair = val[:, (h // 2) * LANES:(h // 2 + 1) * LANES]
            a, b = expand(pair, h % 2)
            dst_ref[:, h * 2 * LANES:h * 2 * LANES + LANES] = a.astype(BF16)
            dst_ref[:, h * 2 * LANES + LANES:(h + 1) * 2 * LANES] = b.astype(BF16)

    qsb_ref[...] = (proj(_C_QSB, _C_KSB) * ATTN_SCALE).astype(BF16)
    ksb = proj(_C_KSB, _C_VSB)
    put_rows(ksb_ref, ksb)
    store_expanded(kesb_ref, ksb, SB_KV_HEADS)
    vsb = proj(_C_VSB, _C_ZSB)
    put_rows(vsb_ref, vsb)
    store_expanded(vesb_ref, vsb, SB_KV_HEADS)
    zsb_ref[...] = proj(_C_ZSB, _C_QD).astype(BF16)

    qd = proj(_C_QD, _C_KD)
    for p in range(DSA_HEADS // 2):
        blk = rope(head_norm(qd[:, p * LANES:(p + 1) * LANES], qn_ref[...]))
        qd_ref[:, p * LANES:(p + 1) * LANES] = (blk * ATTN_SCALE).astype(BF16)
    kd = rope(head_norm(proj(_C_KD, _C_VD), kn_ref[...]))
    put_rows(kd_ref, kd)
    store_expanded(ked_ref, kd, DSA_KV_HEADS)
    vd = proj(_C_VD, _C_ZD)
    put_rows(vd_ref, vd)
    ck = vdt_ref.shape[2]
    for c in range(vdt_ref.shape[0]):
        vdt_ref[c] = vd[c * ck:(c + 1) * ck, :].T.astype(BF16)
    zd_ref[...] = proj(_C_ZD, _C_QI).astype(BF16)

    qi = proj(_C_QI, _C_KI)
    for p in range(IDX_HEADS // 2):
        qi_ref[:, p * LANES:(p + 1) * LANES] = rope(qi[:, p * LANES:(p + 1) * LANES]).astype(BF16)
    kiw = proj(_C_KI, _C_GSB)
    kiw_ref[...] = kiw
    ki = rope(head_norm(kiw, in_ref[...]))
    if feature_major:
        ki_ref[0] = ki.T[:IDX_DIM, :]
    else:
        ki_ref[...] = ki[:, :IDX_DIM]
    kei_ref[:, :LANES] = ki.astype(BF16)
    kei_ref[:, LANES:] = pltpu.roll(ki, HEAD_DIM, 1).astype(BF16)

    gsb_ref[...] = proj(_C_GSB, _C_GD).astype(BF16)
    gd_ref[...] = proj(_C_GD, _C_END).astype(BF16)


def _features(x2d, rows_per_pos_cycle, cos, sin, norm_g, w_r, qn, kn, inn, tm, key_chunk, feature_major):
    R = x2d.shape[0]
    assert tm % key_chunk == 0
    n_pos_blocks = rows_per_pos_cycle // tm
    row = lambda i: (i, 0)
    const = lambda i: (0, 0)
    pos = lambda i: (i % n_pos_blocks, 0)
    f32_widths = dict(ksb=256, vsb=256, kd=128, vd=128, ki=64)

    def out_spec(n):
        if feature_major and n in f32_widths:
            return pl.BlockSpec((1, widths[n], tm), lambda i: (i // n_pos_blocks, 0, i % n_pos_blocks))
        return pl.BlockSpec((tm, widths[n]), row)

    def out_shape(n):
        if feature_major and n in f32_widths:
            return jax.ShapeDtypeStruct((R // rows_per_pos_cycle, widths[n], rows_per_pos_cycle), F32)
        return jax.ShapeDtypeStruct((R, widths[n]), dtypes[n])

    names = ["ksb", "vsb", "kd", "vd", "ki", "qsb", "zsb", "qd", "zd", "qi", "kiw", "gsb", "gd",
             "kesb", "vesb", "ked", "kei"]
    widths = dict(f32_widths, qsb=512, zsb=512, qd=512, zd=512, qi=512, kiw=128, gsb=1024, gd=1024,
                  kesb=1024, vesb=1024, ked=512, kei=256)
    dtypes = {n: F32 for n in names}
    for n in ("qsb", "qd", "qi", "kesb", "vesb", "ked", "kei", "zsb", "zd", "gsb", "gd"):
        dtypes[n] = BF16
    vdt_rows = DSA_KV_HEADS * HEAD_DIM
    outs = pl.pallas_call(
        functools.partial(_feat_kernel, feature_major=feature_major),
        grid=(R // tm,),
        in_specs=[
            pl.BlockSpec((tm, D_MODEL), row),
            pl.BlockSpec((1, D_MODEL), const),
            pl.BlockSpec((D_MODEL, _C_GSB), const, pipeline_mode=pl.Buffered(1)),
            pl.BlockSpec((D_MODEL, _C_END - _C_GSB), const, pipeline_mode=pl.Buffered(1)),
            pl.BlockSpec((tm, LANES), pos),
            pl.BlockSpec((tm, LANES), pos),
            pl.BlockSpec((1, LANES), const),
            pl.BlockSpec((1, LANES), const),
            pl.BlockSpec((1, LANES), const),
        ],
        out_specs=[out_spec(n) for n in names]
        + [pl.BlockSpec((tm // key_chunk, vdt_rows, key_chunk), lambda i: (i, 0, 0))],
        out_shape=[out_shape(n) for n in names]
        + [jax.ShapeDtypeStruct((R // key_chunk, vdt_rows, key_chunk), BF16)],
        compiler_params=pltpu.CompilerParams(dimension_semantics=("arbitrary",),
                                             vmem_limit_bytes=VMEM_LIMIT),
        name="features",
    )(x2d, norm_g, *w_r, cos, sin, qn, kn, inn)
    return dict(zip(names + ["vdt"], outs))


def _sb_softplus_parts(z):
    e = jnp.exp(-jnp.abs(z))
    return jnp.maximum(z, 0.0) + jnp.log(1.0 + e)


def _sb_kernel(q_ref, ke_ref, ve_ref, o_ref, carry_ref, acc_ref, *, T):
    i = pl.program_id(1)
    rr = lax.broadcasted_iota(I32, (T, T), 0)
    cc = lax.broadcasted_iota(I32, (T, T), 1)
    suffix = (rr > cc).astype(BF16)
    suffix2 = jnp.concatenate([suffix, suffix], axis=0)
    allowed1 = cc < rr
    allowed = jnp.concatenate([allowed1, allowed1], axis=0)

    def widen(v):
        return v if T == LANES else jnp.concatenate([v] * (T // LANES), axis=1)

    def block(j, diag):
        start = pl.multiple_of(j * T, T)
        heads = range(SB_KV_HEADS)
        zs = []
        for kvh in heads:
            q2 = q_ref[:, kvh * LANES:(kvh + 1) * LANES]
            c0 = kvh * 2 * LANES
            ka = ke_ref[pl.ds(start, T), c0:c0 + LANES]
            kb = ke_ref[pl.ds(start, T), c0 + LANES:c0 + 2 * LANES]
            zs.append(jnp.concatenate([_dot_nt(q2, ka), _dot_nt(q2, kb)], axis=0))
        l1ms = [-_sb_softplus_parts(z) for z in zs]
        if diag:
            l1ms = [jnp.where(allowed, l1m, 0.0) for l1m in l1ms]
        tails = [_dot(jnp.concatenate(_split_bf16(l1m), axis=1), suffix2) for l1m in l1ms]
        tots = [jnp.sum(l1m, axis=1, keepdims=True) for l1m in l1ms]
        live = None
        for kvh in heads:
            c0 = kvh * 2 * LANES
            if diag:
                logit = zs[kvh] + l1ms[kvh] + tails[kvh]
                a = jnp.where(allowed, jnp.exp(logit), 0.0).astype(BF16)
                carry = jnp.broadcast_to(tots[kvh], (2 * T, LANES))
            else:
                logit = zs[kvh] + l1ms[kvh] + tails[kvh] + widen(carry_ref[kvh])
                a = jnp.exp(logit).astype(BF16)
                carry = carry_ref[kvh] + tots[kvh]
            va = ve_ref[pl.ds(start, T), c0:c0 + LANES]
            vb = ve_ref[pl.ds(start, T), c0 + LANES:c0 + 2 * LANES]
            o = _dot(a[:T], va) + _dot(a[T:], vb)
            acc_ref[kvh] = o if diag else acc_ref[kvh] + o
            carry_ref[kvh] = carry
            live = carry if live is None else jnp.maximum(live, carry)
        return jnp.max(live)

    def cond(s):
        return jnp.logical_and(s[0] >= 0, s[1] >= SB_DEAD)

    def body(s):
        return s[0] - 1, block(s[0], False)

    lax.while_loop(cond, body, (i - 1, block(i, True)))
    for kvh in range(SB_KV_HEADS):
        o_ref[:, kvh * LANES:(kvh + 1) * LANES] = acc_ref[kvh].astype(o_ref.dtype)


def _sb_attention(qsb, kesb, vesb, B, S, T):
    nq = S // T
    return pl.pallas_call(
        functools.partial(_sb_kernel, T=T),
        grid=(B, nq),
        in_specs=[
            pl.BlockSpec((T, 512), lambda b, i: (b * nq + i, 0)),
            pl.BlockSpec((S, 1024), lambda b, i: (b, 0)),
            pl.BlockSpec((S, 1024), lambda b, i: (b, 0)),
        ],
        out_specs=pl.BlockSpec((T, 512), lambda b, i: (b * nq + i, 0)),
        out_shape=jax.ShapeDtypeStruct((B * S, 512), BF16),
        scratch_shapes=[pltpu.VMEM((SB_KV_HEADS, 2 * T, LANES), F32),
                        pltpu.VMEM((SB_KV_HEADS, T, LANES), F32)],
        compiler_params=pltpu.CompilerParams(dimension_semantics=("arbitrary", "arbitrary"),
                                             vmem_limit_bytes=VMEM_LIMIT),
        name="sb_prompt",
    )(qsb, kesb, vesb)


NEG_INF = float("-inf")
COUNT_WIDE = 2


def _canonical_score(score):
    return jnp.where(score == 0.0, 0.0, score)


def _pattern_to_float(pat):
    return pltpu.bitcast(pat ^ ((pat >> 31) & 0x7FFFFFFF), F32)


def _count_rows(score_ref, n_steps, rows, pred):
    width = score_ref.shape[1]

    def body(j, acc):
        start = pl.multiple_of(j * rows, rows)
        scores = score_ref[pl.ds(start, rows), :]
        idx = start + lax.broadcasted_iota(I32, (rows, width), 0)
        hit = jnp.where(pred(scores, idx), 1, 0).astype(I32)
        return acc + hit.reshape(rows // 8, 8, width).sum(axis=0)

    acc = lax.fori_loop(0, n_steps, body, jnp.zeros((8, width), I32))
    return acc.sum(axis=0, keepdims=True)


def _count_rounded(round_ref, n_steps, rows, cand):
    width = round_ref.shape[1]
    one, zero = jnp.ones((), BF16), jnp.zeros((), BF16)

    def body(j, acc):
        start = pl.multiple_of(j * rows, rows)
        hit = jnp.where(round_ref[pl.ds(start, rows), :] >= cand, one, zero)
        parts = [hit[r:r + 16, :] for r in range(0, rows, 16)]
        while len(parts) > 1:
            parts = [a + b for a, b in zip(parts[::2], parts[1::2])]
        return acc + parts[0].astype(F32)

    acc = lax.fori_loop(0, n_steps, body, jnp.zeros((16, width), F32))
    return acc.sum(axis=0, keepdims=True)


def _kth_largest_score(score_ref, round_ref, n_chunks, chunk, topk):
    width = score_ref.shape[1]
    n_steps = (n_chunks + COUNT_WIDE - 1) // COUNT_WIDE
    rows = COUNT_WIDE * chunk

    def coarse_step(k, p16):
        cand = p16 + jnp.left_shift(jnp.int32(1), 15 - k)
        bits16 = (cand ^ ((cand >> 31) & 0x7FFF)) & 0xFFFF
        cand_b = pltpu.bitcast(bits16 << 16, F32).astype(BF16)
        cnt = _count_rounded(round_ref, n_steps, rows, cand_b)
        return jnp.where(cnt >= topk, cand, p16)

    p16 = lax.fori_loop(0, 16, coarse_step, jnp.full((1, width), -32768, I32))
    few = p16 == -32768
    below = jnp.where(few, 0, p16 - 1)
    base = jnp.where(below >= 0, below * 65536, below * 65536 + 65535)

    def fine_step(k, off):
        cand_off = off | jnp.left_shift(jnp.int32(1), 16 - k)
        cand_f = _pattern_to_float(base + cand_off)
        cnt = _count_rows(score_ref, n_steps, rows, lambda s, idx: s >= cand_f)
        return jnp.where(cnt >= topk, cand_off, off)

    off = lax.fori_loop(0, 17, fine_step, jnp.zeros((1, width), I32))
    t = jnp.where(few, NEG_INF, _pattern_to_float(base + off))
    n_gt = _count_rows(score_ref, n_steps, rows, lambda s, idx: s > t)
    return t, jnp.where(few, 0, topk - n_gt)


def _tie_cutoff(score_ref, n_chunks, chunk, t, need, idx_bits):
    width = score_ref.shape[1]
    n_steps = (n_chunks + COUNT_WIDE - 1) // COUNT_WIDE
    rows = COUNT_WIDE * chunk

    def idx_step(k, c):
        cand = c + jnp.left_shift(jnp.int32(1), idx_bits - 1 - k)
        cnt = _count_rows(score_ref, n_steps, rows,
                          lambda s, idx: jnp.logical_and(s == t, idx < cand))
        return jnp.where(cnt < need, cand, c)

    c = lax.fori_loop(0, idx_bits, idx_step, jnp.zeros((1, width), I32))
    return jnp.where(need > 0, c, -1)


def _dsa_kernel(qi_ref, kiw_ref, kei_ref, qd_ref, ked_ref, vdt_ref, o_ref,
                score_ref, round_ref, acc_ref, kmax_ref, *, T, CK, topk):
    i = pl.program_id(1)
    n_full = (i * T) // CK
    n_chunks = n_full + 1
    rr = lax.broadcasted_iota(I32, (CK, T), 0)
    cc = lax.broadcasted_iota(I32, (CK, T), 1)
    w_t = kiw_ref[...].T[IDX_DIM:IDX_DIM + IDX_HEADS, :]

    def score_chunk(j, last):
        start = pl.multiple_of(j * CK, CK)
        ka = kei_ref[pl.ds(start, CK), :LANES]
        kb = kei_ref[pl.ds(start, CK), LANES:]
        prods = []
        for p in range(IDX_HEADS // 2):
            qp = qi_ref[:, p * LANES:(p + 1) * LANES]
            prods += [_dot_nt(ka, qp), _dot_nt(kb, qp)]
        sc = jnp.zeros((CK, T), F32)
        for h, s in enumerate(prods):
            sc = sc + w_t[h:h + 1, :] * jnp.maximum(s, 0.0)
        sc = _canonical_score(sc * IDX_SCALE)
        if last:
            sc = jnp.where(start + rr <= i * T + cc, sc, NEG_INF)
        score_ref[pl.ds(start, CK), :] = sc
        round_ref[pl.ds(start, CK), :] = sc.astype(BF16)

    def score_body(j, carry):
        score_chunk(j, False)
        return carry

    lax.fori_loop(0, n_full, score_body, 0)
    score_chunk(n_full, True)

    @pl.when(n_chunks % COUNT_WIDE != 0)
    def _():
        pad_start = pl.multiple_of(n_chunks * CK, CK)
        score_ref[pl.ds(pad_start, CK), :] = jnp.full((CK, T), NEG_INF, F32)
        round_ref[pl.ds(pad_start, CK), :] = jnp.full((CK, T), NEG_INF, BF16)

    t, need = _kth_largest_score(score_ref, round_ref, n_chunks, CK, topk)
    need_f = need.astype(F32)

    acc_ref[...] = jnp.zeros(acc_ref.shape, F32)
    top_rows = lax.broadcasted_iota(I32, (LANES, T), 0) < HEAD_DIM
    r2 = lax.broadcasted_iota(I32, (CK, CK), 0)
    c2 = lax.broadcasted_iota(I32, (CK, CK), 1)
    prefix = (c2 <= r2).astype(BF16)

    def chunk_inputs(j, tied_before):
        start = pl.multiple_of(j * CK, CK)
        sc = score_ref[pl.ds(start, CK), :]
        tied = sc == t
        rank = tied_before + _dot(prefix, jnp.where(tied, 1.0, 0.0).astype(BF16))
        sel = jnp.logical_or(sc > t, jnp.logical_and(tied, rank <= need_f))
        return start, sel, rank[CK - 1:CK, :]

    def pair_inputs(j, start, p):
        kvh = p // 2
        kcol = kvh * 2 * LANES
        qp = qd_ref[:, p * LANES:(p + 1) * LANES]
        vt = vdt_ref[j, kvh * HEAD_DIM:(kvh + 1) * HEAD_DIM, :]
        kblks = [ked_ref[pl.ds(start, CK), kcol + e * LANES:kcol + (e + 1) * LANES] for e in range(2)]
        return qp, vt, kblks

    def attend_online(j, state):
        ms, ls, tied_before = state
        start, sel, tied_before = chunk_inputs(j, tied_before)
        new_ms, new_ls = [], []
        for p in range(DSA_HEADS // 2):
            qp, vt, kblks = pair_inputs(j, start, p)
            outs, alphas = [], []
            for e in range(2):
                h = 2 * p + e
                lg = jnp.where(sel, _dot_nt(kblks[e], qp), NEG_BIG)
                m_new = jnp.maximum(ms[h], jnp.max(lg, axis=0, keepdims=True))
                alpha = jnp.exp(ms[h] - m_new)
                pr = jnp.where(sel, jnp.exp(lg - m_new), 0.0)
                new_ls.append(alpha * ls[h] + jnp.sum(pr, axis=0, keepdims=True))
                new_ms.append(m_new)
                outs.append(_dot(vt, pr.astype(BF16)))
                alphas.append(alpha)
            scale = jnp.where(top_rows, alphas[0], alphas[1])
            acc_ref[p] = acc_ref[p] * scale + jnp.concatenate(outs, axis=0)
        return tuple(new_ms), tuple(new_ls), tied_before

    def softmax_online(_):
        init = (tuple(jnp.full((1, T), NEG_BIG, F32) for _ in range(DSA_HEADS)),
                tuple(jnp.zeros((1, T), F32) for _ in range(DSA_HEADS)),
                jnp.zeros((1, T), F32))
        return lax.fori_loop(0, n_chunks, attend_online, init)[1]

    def attend_unshifted(j, state):
        ls, tied_before = state
        start, sel, tied_before = chunk_inputs(j, tied_before)
        pairs = [pair_inputs(j, start, p) for p in range(DSA_HEADS // 2)]
        lgs = [_dot_nt(kblks[e], qp) for qp, _, kblks in pairs for e in range(2)]
        prs = [jnp.where(sel, jnp.exp(lg), 0.0) for lg in lgs]
        new_ls = tuple(l + pr.reshape(CK // 8, 8, T).sum(axis=0) for l, pr in zip(ls, prs))
        for p, (_, vt, _) in enumerate(pairs):
            outs = [_dot(vt, prs[2 * p + e].astype(BF16)) for e in range(2)]
            acc_ref[p] = acc_ref[p] + jnp.concatenate(outs, axis=0)
        return new_ls, tied_before

    def softmax_unshifted(_):
        init = (tuple(jnp.zeros((8, T), F32) for _ in range(DSA_HEADS)), jnp.zeros((1, T), F32))
        ls = lax.fori_loop(0, n_chunks, attend_unshifted, init)[0]
        return tuple(l.sum(axis=0, keepdims=True) for l in ls)

    @pl.when(i == 0)
    def _():
        for kvh in range(DSA_KV_HEADS):
            def k_norm(c, m):
                rows = ked_ref[pl.ds(pl.multiple_of(c * CK, CK), CK),
                               kvh * 2 * LANES:kvh * 2 * LANES + LANES].astype(F32)
                return jnp.maximum(m, jnp.max(jnp.sum(rows * rows, axis=1, keepdims=True)))
            kmax_ref[kvh] = lax.fori_loop(0, ked_ref.shape[0] // CK, k_norm, jnp.float32(0.0))

    bound2 = jnp.float32(0.0)
    for p in range(DSA_HEADS // 2):
        qf = qd_ref[:, p * LANES:(p + 1) * LANES].astype(F32)
        q2max = jnp.max(jnp.sum(qf * qf, axis=1, keepdims=True))
        bound2 = jnp.maximum(bound2, q2max * kmax_ref[p // 2])
    ls = lax.cond(bound2 <= LOGIT_SAFE * LOGIT_SAFE, softmax_unshifted, softmax_online, 0)
    for p in range(DSA_HEADS // 2):
        denom = jnp.where(top_rows, ls[2 * p], ls[2 * p + 1])
        o_ref[:, p * LANES:(p + 1) * LANES] = (acc_ref[p] / denom).T.astype(o_ref.dtype)


def _dsa_attention(qi, kiw, kei, qd, ked, vdt, B, S, T, topk):
    nq = S // T
    CK = vdt.shape[2]
    assert S % (COUNT_WIDE * CK) == 0 and CK % T == 0
    qrow = lambda b, i: (b * nq + i, 0)
    batch = lambda b, i: (b, 0)
    return pl.pallas_call(
        functools.partial(_dsa_kernel, T=T, CK=CK, topk=topk),
        grid=(B, nq),
        in_specs=[
            pl.BlockSpec((T, 512), qrow),
            pl.BlockSpec((T, LANES), qrow),
            pl.BlockSpec((S, 256), batch),
            pl.BlockSpec((T, 512), qrow),
            pl.BlockSpec((S, 512), batch),
            pl.BlockSpec((S // CK, LANES, CK), lambda b, i: (b, 0, 0)),
        ],
        out_specs=pl.BlockSpec((T, 512), qrow),
        out_shape=jax.ShapeDtypeStruct((B * S, 512), BF16),
        scratch_shapes=[
            pltpu.VMEM((S, T), F32),
            pltpu.VMEM((S, T), BF16),
            pltpu.VMEM((DSA_HEADS // 2, LANES, T), F32),
            pltpu.SMEM((DSA_KV_HEADS,), F32),
        ],
        compiler_params=pltpu.CompilerParams(dimension_semantics=("arbitrary", "arbitrary"),
                                             vmem_limit_bytes=VMEM_LIMIT),
        name="dsa_prompt",
    )(qi, kiw, kei, qd, ked, vdt)


def _merge_kernel(x_ref, osb_ref, zsb_ref, od_ref, zd_ref, gsb_ref, gd_ref,
                  wsb_ref, wd_ref, wo_ref, y_ref):
    zsb = zsb_ref[...].astype(F32)
    zd = zd_ref[...].astype(F32)
    a_sb = (osb_ref[...] * (zsb * jax.nn.sigmoid(zsb))).astype(BF16)
    a_d = (od_ref[...] * (zd * jax.nn.sigmoid(zd))).astype(BF16)
    u_sb = _dot(a_sb, wsb_ref[...])
    u_d = _dot(a_d, wd_ref[...])
    mixed = (jax.nn.sigmoid(gsb_ref[...].astype(F32)) * u_sb
             + jax.nn.sigmoid(gd_ref[...].astype(F32)) * u_d)
    y_ref[...] = x_ref[...] + _dot(mixed.astype(BF16), wo_ref[...])


def _merge(x2d, osb, zsb, od, zd, gsb, gd, wsb, wd, wo, tm):
    R = x2d.shape[0]
    row = lambda i: (i, 0)
    const = lambda i: (0, 0)
    return pl.pallas_call(
        _merge_kernel,
        grid=(R // tm,),
        in_specs=[
            pl.BlockSpec((tm, D_MODEL), row),
            pl.BlockSpec((tm, 512), row),
            pl.BlockSpec((tm, 512), row),
            pl.BlockSpec((tm, 512), row),
            pl.BlockSpec((tm, 512), row),
            pl.BlockSpec((tm, D_MODEL), row),
            pl.BlockSpec((tm, D_MODEL), row),
            pl.BlockSpec((512, D_MODEL), const),
            pl.BlockSpec((512, D_MODEL), const),
            pl.BlockSpec((D_MODEL, D_MODEL), const),
        ],
        out_specs=pl.BlockSpec((tm, D_MODEL), row),
        out_shape=jax.ShapeDtypeStruct((R, D_MODEL), F32),
        compiler_params=pltpu.CompilerParams(dimension_semantics=("arbitrary",),
                                             vmem_limit_bytes=VMEM_LIMIT),
        name="merge",
    )(x2d, osb, zsb, od, zd, gsb, gd, wsb, wd, wo)


def _rope_tables(pos):
    inv = ROPE_THETA ** (-jnp.arange(HALF, dtype=F32) / HALF)
    ang = pos.astype(F32)[:, None] * inv[None, :]
    cos = jnp.cos(ang)
    sin = jnp.sin(ang)
    return jnp.tile(cos, (1, 4)), jnp.tile(jnp.concatenate([-sin, sin], axis=1), (1, 2))


def _prep_weights(norm_g, w_in, q_norm_g, k_norm_g, idx_k_norm_g, w_up_sb, w_up_dsa, w_out):
    w_r = (jnp.pad(w_in[:, :_W_SPLIT].astype(BF16), ((0, 0), (0, _C_GSB - _W_SPLIT))),
           w_in[:, _W_SPLIT:].astype(BF16))
    qn = jnp.tile(q_norm_g, 2)[None, :]
    kn = jnp.tile(k_norm_g, 2)[None, :]
    inn = jnp.concatenate([idx_k_norm_g, jnp.zeros((LANES - IDX_DIM,), F32)])[None, :]
    return (norm_g[None, :], w_r, qn, kn, inn,
            w_up_sb.astype(BF16), w_up_dsa.astype(BF16), w_out.astype(BF16))


def _prompt_layer(x, weights, tiles):
    B, S, _ = x.shape
    t_feat, t_merge, t_sb, t_dsa, key_chunk = tiles
    norm_g, w_r, qn, kn, inn, wsb, wd, wo = weights
    x2d = x.reshape(B * S, D_MODEL)
    cos, sin = _rope_tables(jnp.arange(S))
    f = _features(x2d, S, cos, sin, norm_g, w_r, qn, kn, inn, t_feat, key_chunk, True)
    topk = min(DSA_TOPK_MAX, S // 4)
    osb = _sb_attention(f["qsb"], f["kesb"], f["vesb"], B, S, t_sb)
    od = _dsa_attention(f["qi"], f["kiw"], f["kei"], f["qd"], f["ked"], f["vdt"], B, S, t_dsa, topk)
    y = _merge(x2d, osb, f["zsb"], od, f["zd"], f["gsb"], f["gd"], wsb, wd, wo, t_merge)

    def rows(a, heads):
        return jnp.moveaxis(a.reshape(B, heads, HEAD_DIM, S), 3, 1)[None]

    return (y.reshape(B, S, D_MODEL),
            rows(f["ksb"], SB_KV_HEADS), rows(f["vsb"], SB_KV_HEADS),
            rows(f["kd"], DSA_KV_HEADS), rows(f["vd"], DSA_KV_HEADS),
            jnp.moveaxis(f["ki"], 2, 1)[None])


PAGE_UNROLL = 64


def _page_copy(cache_hbm, pt_ref, buf, sems, ci, seq, slot, p):
    return pltpu.make_async_copy(cache_hbm.at[pt_ref[seq, p]], buf.at[slot, p], sems.at[ci, slot])


def _stream_pages(caches, pt_ref, bufs, sems, n_pages):
    b = pl.program_id(0)
    nb = pl.num_programs(0)
    slot = b % 2

    def start_all_pages(seq, slot_):
        def body(p, carry):
            for ci, (cache, buf) in enumerate(zip(caches, bufs)):
                _page_copy(cache, pt_ref, buf, sems, ci, seq, slot_, p).start()
            return carry
        lax.fori_loop(0, n_pages, body, 0, unroll=min(8, n_pages))

    @pl.when(b == 0)
    def _():
        start_all_pages(0, 0)

    @pl.when(b + 1 < nb)
    def _():
        start_all_pages(b + 1, 1 - slot)

    for p in range(n_pages):
        for ci, (cache, buf) in enumerate(zip(caches, bufs)):
            _page_copy(cache, pt_ref, buf, sems, ci, b, slot, p).wait()
    return slot


def _dec_sb_kernel(pt_ref, qm_ref, k_hbm, v_hbm, o_ref, kfirst, vfirst, kmore, vmore, sems, *, n_pages, G):
    b = pl.program_id(0)
    nb = pl.num_programs(0)
    slot = b % 2
    n_groups = n_pages // G
    qm = qm_ref[0]
    P = PAGE_SIZE
    rr = lax.broadcasted_iota(I32, (P, P), 0)
    cc = lax.broadcasted_iota(I32, (P, P), 1)
    suffix = (rr > cc).astype(BF16)
    ones = jnp.ones((P, LANES), BF16)

    def group_copies(seq, g, kdst, vdst, ksem, vsem):
        copies = []
        for r in range(G):
            page = pt_ref[seq, n_pages - (g + 1) * G + r]
            copies.append(pltpu.make_async_copy(k_hbm.at[page], kdst.at[r], ksem))
            copies.append(pltpu.make_async_copy(v_hbm.at[page], vdst.at[r], vsem))
        return copies

    def first_copies(seq, s):
        return group_copies(seq, 0, kfirst.at[s], vfirst.at[s], sems.at[0, s], sems.at[1, s])

    def process(kb, vb, carry, acc):
        zs, l1ms, tails, tots = [], [], [], []
        for r in range(G):
            z = _dot(qm, kb[r].astype(BF16))
            l1m = -_sb_softplus_parts(z)
            hi, lo = _split_bf16(l1m)
            zs.append(z)
            l1ms.append(l1m)
            tails.append(_dot(hi, suffix) + _dot(lo, suffix))
            tots.append(_dot(hi, ones) + _dot(lo, ones))
        for r in reversed(range(G)):
            a = jnp.exp(zs[r] + l1ms[r] + tails[r] + carry).astype(BF16)
            acc = acc + _dot_nt(a, vb[r].astype(BF16))
            carry = carry + tots[r]
        return carry, acc

    @pl.when(b == 0)
    def _():
        for c in first_copies(0, 0):
            c.start()

    @pl.when(b + 1 < nb)
    def _():
        for c in first_copies(b + 1, 1 - slot):
            c.start()

    for c in first_copies(b, slot):
        c.wait()
    n_heads = qm.shape[0]
    carry, acc = process(kfirst.at[slot], vfirst.at[slot],
                         jnp.zeros((n_heads, LANES), F32), jnp.zeros((n_heads, qm.shape[1]), F32))

    def cond(s):
        return jnp.logical_and(s[0] < n_groups, s[3] >= SB_DEAD)

    def body(s):
        g, carry, acc, _ = s
        copies = group_copies(b, g, kmore, vmore, sems.at[2, 0], sems.at[2, 1])
        for c in copies:
            c.start()
        for c in copies:
            c.wait()
        carry, acc = process(kmore, vmore, carry, acc)
        return g + 1, carry, acc, jnp.max(carry)

    _, _, acc, _ = lax.while_loop(cond, body, (jnp.int32(1), carry, acc, jnp.max(carry)))
    o_ref[0] = acc


def _dec_sb(page_table, qm, cache_k, cache_v, group):
    DB, n_pages = page_table.shape
    width = cache_k.shape[1]
    return pl.pallas_call(
        functools.partial(_dec_sb_kernel, n_pages=n_pages, G=group),
        grid_spec=pltpu.PrefetchScalarGridSpec(
            num_scalar_prefetch=1,
            grid=(DB,),
            in_specs=[
                pl.BlockSpec((1, SB_HEADS, width), lambda b, pt: (b, 0, 0)),
                pl.BlockSpec(memory_space=pl.ANY),
                pl.BlockSpec(memory_space=pl.ANY),
            ],
            out_specs=pl.BlockSpec((1, SB_HEADS, width), lambda b, pt: (b, 0, 0)),
            scratch_shapes=[
                pltpu.VMEM((2, group, width, PAGE_SIZE), F32),
                pltpu.VMEM((2, group, width, PAGE_SIZE), F32),
                pltpu.VMEM((group, width, PAGE_SIZE), F32),
                pltpu.VMEM((group, width, PAGE_SIZE), F32),
                pltpu.SemaphoreType.DMA((3, 2)),
            ]),
        out_shape=jax.ShapeDtypeStruct((DB, SB_HEADS, width), F32),
        compiler_params=pltpu.CompilerParams(dimension_semantics=("arbitrary",),
                                             vmem_limit_bytes=VMEM_LIMIT),
        name="dec_sb",
    )(page_table, qm, cache_k, cache_v)


def _dec_idx_kernel(pt_ref, qi_ref, w_ref, kinew_ref, k_hbm, score_ref, kbuf, sems, *, n_pages, n_rows):
    slot = _stream_pages((k_hbm,), pt_ref, (kbuf,), sems, n_pages)
    qi = qi_ref[0]
    w = w_ref[0]

    def score(pp, carry):
        kk = jnp.concatenate([kbuf[slot, 2 * pp], kbuf[slot, 2 * pp + 1]], axis=1).astype(BF16)
        s = _dot(qi, kk)
        sc = _canonical_score(jnp.sum(w * jnp.maximum(s, 0.0), axis=0, keepdims=True) * IDX_SCALE)
        score_ref[0, pl.ds(2 * pp, 1), :] = sc[:, :PAGE_SIZE]
        score_ref[0, pl.ds(2 * pp + 1, 1), :] = sc[:, PAGE_SIZE:]
        return carry

    lax.fori_loop(0, n_pages // 2, score, 0, unroll=min(PAGE_UNROLL // 2, n_pages // 2))
    s_new = jnp.sum(qi.astype(F32) * kinew_ref[0].astype(F32), axis=1, keepdims=True)
    sc_new = jnp.sum(w * jnp.maximum(s_new, 0.0), axis=0, keepdims=True) * IDX_SCALE
    tail_rows = n_rows - n_pages
    r = lax.broadcasted_iota(I32, (tail_rows, LANES), 0)
    c = lax.broadcasted_iota(I32, (tail_rows, LANES), 1)
    new_row = jnp.broadcast_to(_canonical_score(sc_new), (tail_rows, LANES))
    score_ref[0, n_pages:, :] = jnp.where(jnp.logical_and(r == 0, c == 0), new_row, NEG_INF)


def _dec_idx(page_table, qi, w, kinew, cache_k, n_rows):
    DB, n_pages = page_table.shape
    per_seq = lambda b, pt: (b, 0, 0)
    return pl.pallas_call(
        functools.partial(_dec_idx_kernel, n_pages=n_pages, n_rows=n_rows),
        grid_spec=pltpu.PrefetchScalarGridSpec(
            num_scalar_prefetch=1,
            grid=(DB,),
            in_specs=[
                pl.BlockSpec((1, IDX_HEADS, IDX_DIM), per_seq),
                pl.BlockSpec((1, IDX_HEADS, 1), per_seq),
                pl.BlockSpec((1, 1, IDX_DIM), per_seq),
                pl.BlockSpec(memory_space=pl.ANY),
            ],
            out_specs=pl.BlockSpec((1, n_rows, LANES), per_seq),
            scratch_shapes=[
                pltpu.VMEM((2, n_pages, IDX_DIM, PAGE_SIZE), F32),
                pltpu.SemaphoreType.DMA((1, 2)),
            ]),
        out_shape=jax.ShapeDtypeStruct((DB, n_rows, LANES), F32),
        compiler_params=pltpu.CompilerParams(dimension_semantics=("arbitrary",),
                                             vmem_limit_bytes=VMEM_LIMIT),
        name="dec_idx",
    )(page_table, qi, w, kinew, cache_k)


def _dec_thr_kernel(scores_ref, t_ref, c_ref, score_scr, round_scr, *, n_rows, topk, idx_bits):
    for p in range(n_rows):
        blk = scores_ref[:, p * LANES:(p + 1) * LANES].T
        score_scr[p * LANES:(p + 1) * LANES, :] = blk
        round_scr[p * LANES:(p + 1) * LANES, :] = blk.astype(BF16)
    n_chunks, chunk = n_rows // 2, 2 * LANES
    t, need = _kth_largest_score(score_scr, round_scr, n_chunks, chunk, topk)
    t_ref[...] = t
    c_ref[...] = _tie_cutoff(score_scr, n_chunks, chunk, t, need, idx_bits)


def _dec_thr(scores2d, n_rows, topk):
    DB = scores2d.shape[0]
    assert n_rows % (2 * COUNT_WIDE) == 0
    idx_bits = (n_rows * LANES - 1).bit_length()
    return pl.pallas_call(
        functools.partial(_dec_thr_kernel, n_rows=n_rows, topk=topk, idx_bits=idx_bits),
        out_shape=[jax.ShapeDtypeStruct((1, DB), F32), jax.ShapeDtypeStruct((1, DB), I32)],
        scratch_shapes=[pltpu.VMEM((n_rows * LANES, DB), F32), pltpu.VMEM((n_rows * LANES, DB), BF16)],
        compiler_params=pltpu.CompilerParams(vmem_limit_bytes=VMEM_LIMIT),
        name="dec_thr",
    )(scores2d)


def _dec_dsa_kernel(pt_ref, c_ref, t_ref, qm_ref, scores_ref, kdnew_ref, vdnew_ref, k_hbm, v_hbm, o_ref,
                    kbuf, vbuf, lbuf, sems, *, n_pages):
    b = pl.program_id(0)
    slot = _stream_pages((k_hbm, v_hbm), pt_ref, (kbuf, vbuf), sems, n_pages)
    qm = qm_ref[0]
    t = t_ref[0]
    cidx = c_ref[b]
    lane = lax.broadcasted_iota(I32, (1, LANES), 1)

    def selected(p):
        sc = scores_ref[0, pl.ds(p, 1), :]
        idx = p * PAGE_SIZE + lane
        return jnp.logical_or(sc > t, jnp.logical_and(sc == t, idx <= cidx))

    def page_pair(buf, pp):
        return jnp.concatenate([buf[slot, 2 * pp], buf[slot, 2 * pp + 1]], axis=1).astype(BF16)

    def logits(pp, m):
        sel = jnp.concatenate([selected(2 * pp), selected(2 * pp + 1)], axis=1)
        lg = jnp.where(sel, _dot(qm, page_pair(kbuf, pp)), NEG_BIG)
        lbuf[pp] = lg
        return jnp.maximum(m, lg)

    n_heads = qm.shape[0]
    n_pairs = n_pages // 2
    unroll = min(PAGE_UNROLL // 2, n_pairs)
    m = lax.fori_loop(0, n_pairs, logits, jnp.full((n_heads, 2 * LANES), NEG_BIG, F32), unroll=unroll)
    sel_new = selected(n_pages)[:, :1]
    lg_new = jnp.sum(qm.astype(F32) * kdnew_ref[0].astype(F32), axis=1, keepdims=True)
    lg_new = jnp.where(sel_new, lg_new, NEG_BIG)
    m = jnp.maximum(jnp.max(m, axis=1, keepdims=True), lg_new)

    def attend(pp, state):
        l, acc = state
        pr = jnp.exp(lbuf[pp] - m)
        return l + pr, acc + _dot_nt(pr.astype(BF16), page_pair(vbuf, pp))

    l, acc = lax.fori_loop(0, n_pairs, attend,
                           (jnp.zeros((n_heads, 2 * LANES), F32), jnp.zeros((n_heads, LANES), F32)),
                           unroll=unroll)
    pr_new = jnp.where(sel_new, jnp.exp(lg_new - m), 0.0)
    pr_new_b = pr_new.astype(BF16).astype(F32)
    acc = acc + pr_new_b * vdnew_ref[0].astype(F32)
    denom = jnp.sum(l, axis=1, keepdims=True) + pr_new
    o_ref[0] = acc / denom


def _dec_dsa(page_table, cidx, t_rep, qm, scores, kdnew, vdnew, cache_k, cache_v):
    DB, n_pages = page_table.shape
    n_rows = scores.shape[1]
    per_seq = lambda b, pt, c: (b, 0, 0)
    return pl.pallas_call(
        functools.partial(_dec_dsa_kernel, n_pages=n_pages),
        grid_spec=pltpu.PrefetchScalarGridSpec(
            num_scalar_prefetch=2,
            grid=(DB,),
            in_specs=[
                pl.BlockSpec((1, 1, LANES), per_seq),
                pl.BlockSpec((1, DSA_HEADS, LANES), per_seq),
                pl.BlockSpec((1, n_rows, LANES), per_seq),
                pl.BlockSpec((1, 1, LANES), per_seq),
                pl.BlockSpec((1, 1, LANES), per_seq),
                pl.BlockSpec(memory_space=pl.ANY),
                pl.BlockSpec(memory_space=pl.ANY),
            ],
            out_specs=pl.BlockSpec((1, DSA_HEADS, LANES), per_seq),
            scratch_shapes=[
                pltpu.VMEM((2, n_pages, PAGE_SIZE, LANES), F32),
                pltpu.VMEM((2, n_pages, PAGE_SIZE, LANES), F32),
                pltpu.VMEM((n_pages // 2, DSA_HEADS, 2 * LANES), F32),
                pltpu.SemaphoreType.DMA((2, 2)),
            ]),
        out_shape=jax.ShapeDtypeStruct((DB, DSA_HEADS, LANES), F32),
        compiler_params=pltpu.CompilerParams(dimension_semantics=("arbitrary",),
                                             vmem_limit_bytes=VMEM_LIMIT),
        name="dec_dsa",
    )(page_table, cidx, t_rep, qm, scores, kdnew, vdnew, cache_k, cache_v)


def _heads_in_kv_lanes(q, n_heads, n_kv):
    R = q.shape[0]
    q3 = q.reshape(R, n_heads, 1, HEAD_DIM)
    kv_of_head = jnp.arange(n_heads) // (n_heads // n_kv)
    onehot = (kv_of_head[:, None] == jnp.arange(n_kv)[None, :])[None, :, :, None]
    return jnp.where(onehot, q3, jnp.zeros((), q.dtype)).reshape(R, n_heads, n_kv * HEAD_DIM)


def _own_kv_lanes(o, n_heads, n_kv):
    R = o.shape[0]
    o4 = o.reshape(R, n_heads, n_kv, HEAD_DIM)
    group = n_heads // n_kv
    return jnp.concatenate([o4[:, h, h // group, :] for h in range(n_heads)], axis=1)


def _sample_layer(x, caches, page_table, weights):
    DB, T, _ = x.shape
    c_sb_k, c_sb_v, c_dsa_k, c_dsa_v, c_idx_k = caches
    norm_g, w_r, qn, kn, inn, wsb, wd, wo = weights
    n_pages = page_table.shape[1]
    past = n_pages * PAGE_SIZE
    x2d = x.reshape(DB * T, D_MODEL)
    cos, sin = _rope_tables(jnp.full((DB * T,), past))
    f = _features(x2d, DB * T, cos, sin, norm_g, w_r, qn, kn, inn, DB * T, DB * T, False)

    n_pool = c_sb_k.shape[0]

    def pages_t(c):
        return jnp.moveaxis(c, 1, -1).reshape(n_pool, -1, PAGE_SIZE)

    qm_sb = _heads_in_kv_lanes(f["qsb"], SB_HEADS, SB_KV_HEADS)
    o_sb = _dec_sb(page_table, qm_sb, pages_t(c_sb_k), pages_t(c_sb_v), 2)
    osb = _own_kv_lanes(o_sb, SB_HEADS, SB_KV_HEADS)

    n_rows = n_pages + 8
    qi3 = f["qi"].reshape(DB, IDX_HEADS, IDX_DIM)
    w3 = f["kiw"][:, IDX_DIM:IDX_DIM + IDX_HEADS].reshape(DB, IDX_HEADS, 1)
    kinew = f["kei"][:, :IDX_DIM].reshape(DB, 1, IDX_DIM)
    scores = _dec_idx(page_table, qi3, w3, kinew, pages_t(c_idx_k), n_rows)
    topk = min(DSA_TOPK_MAX, (past + T) // 4)
    t, cidx = _dec_thr(scores.reshape(DB, n_rows * LANES), n_rows, topk)
    t_rep = jnp.broadcast_to(t.reshape(DB, 1, 1), (DB, 1, LANES))

    qm_d = _heads_in_kv_lanes(f["qd"], DSA_HEADS, DSA_KV_HEADS)
    kdnew = f["kd"].astype(BF16).reshape(DB, 1, LANES)
    vdnew = f["vd"].astype(BF16).reshape(DB, 1, LANES)
    o_d = _dec_dsa(page_table, cidx.reshape(DB), t_rep, qm_d, scores, kdnew, vdnew,
                   pages_t(c_dsa_k), pages_t(c_dsa_v))
    od = _own_kv_lanes(o_d, DSA_HEADS, DSA_KV_HEADS)

    y = _merge(x2d, osb, f["zsb"], od, f["zd"], f["gsb"], f["gd"], wsb, wd, wo, DB * T)
    return (y.reshape(DB, T, D_MODEL),
            f["ksb"].reshape(1, DB, T, SB_KV_HEADS, HEAD_DIM),
            f["vsb"].reshape(1, DB, T, SB_KV_HEADS, HEAD_DIM),
            f["kd"].reshape(1, DB, T, DSA_KV_HEADS, HEAD_DIM),
            f["vd"].reshape(1, DB, T, DSA_KV_HEADS, HEAD_DIM),
            f["ki"].reshape(1, DB, T, IDX_DIM))


def kernel(x_prompt, x_sample, cache_sb_k, cache_sb_v, cache_dsa_k, cache_dsa_v, cache_idx_k, page_table,
           norm_g, w_in, q_norm_g, k_norm_g, idx_k_norm_g, w_up_sb, w_up_dsa, w_out):
    assert norm_g.shape[0] == 1 and x_sample.shape[1] == 1
    weights = _prep_weights(norm_g[0], w_in[0], q_norm_g[0], k_norm_g[0], idx_k_norm_g[0],
                            w_up_sb[0], w_up_dsa[0], w_out[0])
    p = _prompt_layer(x_prompt, weights, PROMPT_TILES)
    caches = (cache_sb_k[0], cache_sb_v[0], cache_dsa_k[0], cache_dsa_v[0], cache_idx_k[0])
    s = _sample_layer(x_sample, caches, page_table, weights)
    return (p[0], s[0], p[1], p[2], p[3], p[4], p[5], s[1], s[2], s[3], s[4], s[5])
```
